```python
import math
import jax, jax.numpy as jnp
from jax import lax
import numpy as np

D_MODEL = 2048
BATCH = 4
SEQ = 2048
DEPTH = 4
DEC_BATCH = 128
DEC_SEQ = 8
PAST_LEN = 16384
PAGE_SIZE = 128

D_SSM = D_MODEL // 2
SSM_GROUP = 16
N_SSM_GROUPS = D_SSM // SSM_GROUP
SSM_STATE = 64
D_CONV = D_MODEL // 2
CONV_WIDTH = 31
D_IN_PROJ = D_SSM + 2 * D_CONV + 2 * D_MODEL
N_EXPERT_GROUPS = 4
EXPERTS_PER_GROUP = 8
N_EXPERTS = N_EXPERT_GROUPS * EXPERTS_PER_GROUP
TOP_K_IN_GROUP = 2
D_EXPERT = D_MODEL // 8
ALPHA = (2 * DEPTH) ** 0.25
BETA = (8 * DEPTH) ** -0.25
LN_EPS = 1e-5
ADA_CHUNKS = 6

kernel_name = "gated_s5_conformer_hmoe_decoder_step"


def _layer_norm(x, g=None, b=None):
    xf = x.astype(jnp.float32)
    mu = jnp.mean(xf, axis=-1, keepdims=True)
    var = jnp.mean(jnp.square(xf - mu), axis=-1, keepdims=True)
    y = (xf - mu) * lax.rsqrt(var + LN_EPS)
    if g is not None:
        y = y * g.astype(jnp.float32) + b.astype(jnp.float32)
    return y.astype(x.dtype)


def _complex_affine_combine(e1, e2):
    ar1, ai1, br1, bi1 = e1
    ar2, ai2, br2, bi2 = e2
    return (ar1 * ar2 - ai1 * ai2,
            ar1 * ai2 + ai1 * ar2,
            ar2 * br1 - ai2 * bi1 + br2,
            ar2 * bi1 + ai2 * br1 + bi2)


def _s5_scan(u, h0_re, h0_im, a_re, a_im, log_dt, b_re, b_im, c_re, c_im, d_skip):
    f32 = jnp.float32
    bsz, seq, _ = u.shape
    uf = u.astype(f32).reshape(bsz, seq, N_SSM_GROUPS, SSM_GROUP)
    a_re = a_re.astype(f32)
    a_im = a_im.astype(f32)
    dt = jnp.exp(log_dt.astype(f32))[:, None]
    mag = jnp.exp(dt * a_re)
    ang = dt * a_im
    ab_re = mag * jnp.cos(ang)
    ab_im = mag * jnp.sin(ang)
    den = jnp.square(a_re) + jnp.square(a_im)
    k_re = ((ab_re - 1.0) * a_re + ab_im * a_im) / den
    k_im = (ab_im * a_re - (ab_re - 1.0) * a_im) / den
    b_re = b_re.astype(f32)
    b_im = b_im.astype(f32)
    bb_re = k_re[..., None] * b_re - k_im[..., None] * b_im
    bb_im = k_re[..., None] * b_im + k_im[..., None] * b_re
    bu_re = jnp.einsum('bsgk,gpk->bsgp', uf, bb_re)
    bu_im = jnp.einsum('bsgk,gpk->bsgp', uf, bb_im)
    h0_re = h0_re.astype(f32)
    h0_im = h0_im.astype(f32)
    bu_re = bu_re.at[:, 0].add(ab_re * h0_re - ab_im * h0_im)
    bu_im = bu_im.at[:, 0].add(ab_re * h0_im + ab_im * h0_re)
    ar = jnp.broadcast_to(ab_re, bu_re.shape)
    ai = jnp.broadcast_to(ab_im, bu_re.shape)
    _, _, s_re, s_im = lax.associative_scan(_complex_affine_combine, (ar, ai, bu_re, bu_im), axis=1)
    y = (jnp.einsum('bsgp,gkp->bsgk', s_re, c_re.astype(f32))
         - jnp.einsum('bsgp,gkp->bsgk', s_im, c_im.astype(f32))
         + d_skip.astype(f32).reshape(N_SSM_GROUPS, SSM_GROUP) * uf)
    return (y.reshape(bsz, seq, D_SSM).astype(u.dtype),
            s_re[:, -1].astype(u.dtype), s_im[:, -1].astype(u.dtype))


def _causal_depthwise_conv(u, buf, w, b):
    padded = jnp.concatenate([buf.astype(u.dtype), u], axis=1)
    y = lax.conv_general_dilated(padded, w[:, None, :].astype(u.dtype), window_strides=(1,),
                                 padding='VALID', dimension_numbers=('NWC', 'WIO', 'NWC'),
                                 feature_group_count=D_CONV)
    return y + b, padded[:, -(CONV_WIDTH - 1):]


def _hier_moe(h, w_group, b_group, w_router, b_router, w_up, w_gate, w_down):
    f32 = jnp.float32
    bsz, seq, d = h.shape
    hf = h.reshape(bsz * seq, d)
    glog = (hf @ w_group).astype(f32) + b_group.astype(f32)
    gprob = jax.nn.softmax(glog, axis=-1)
    gsel = jnp.argmax(glog, axis=-1)
    p_g = jnp.take_along_axis(gprob, gsel[:, None], axis=1)
    elog = ((hf @ w_router).astype(f32) + b_router.astype(f32)).reshape(-1, N_EXPERT_GROUPS, EXPERTS_PER_GROUP)
    sel_idx = jnp.broadcast_to(gsel[:, None, None], (elog.shape[0], 1, EXPERTS_PER_GROUP))
    elog_sel = jnp.take_along_axis(elog, sel_idx, axis=1)[:, 0]
    topv, topi = lax.top_k(elog_sel, TOP_K_IN_GROUP)
    wts = jax.nn.softmax(topv, axis=-1) * p_g
    expert_idx = gsel[:, None] * EXPERTS_PER_GROUP + topi
    combine = jnp.sum(jax.nn.one_hot(expert_idx, N_EXPERTS, dtype=f32) * wts[..., None], axis=1)
    up = jnp.einsum('td,edf->tef', hf, w_up)
    gt = jnp.einsum('td,edf->tef', hf, w_gate)
    act = jax.nn.silu(gt) * up * combine.astype(h.dtype)[..., None]
    y = jnp.einsum('tef,efd->td', act, w_down)
    return y.reshape(bsz, seq, d)


def _layer(x, c, h0_re, h0_im, conv_buf,
           w_ada, b_ada, w_in, ssm_a_re, ssm_a_im, ssm_log_dt, ssm_b_re, ssm_b_im, ssm_c_re, ssm_c_im,
           ssm_d, w_s5_val, w_s5_gate, conv_w, conv_b, conv_ln_g, conv_ln_b, w_conv_pw, w_out,
           ln1_g, ln1_b, moe_w_group, moe_b_group, moe_w_router, moe_b_router,
           moe_w_up, moe_w_gate, moe_w_down, ln2_g, ln2_b):
    ada = jax.nn.silu(c) @ w_ada + b_ada
    sh1, sc1, g1, sh2, sc2, g2 = jnp.split(ada[:, None, :], ADA_CHUNKS, axis=-1)

    h = _layer_norm(x) * (1.0 + sc1) + sh1
    proj = h @ w_in
    u_s = proj[..., :D_SSM]
    u_c2 = proj[..., D_SSM:D_SSM + 2 * D_CONV]
    gates = jax.nn.sigmoid(proj[..., D_SSM + 2 * D_CONV:])
    g_s, g_c = gates[..., :D_MODEL], gates[..., D_MODEL:]
    y_s, hT_re, hT_im = _s5_scan(u_s, h0_re, h0_im, ssm_a_re, ssm_a_im, ssm_log_dt,
                                 ssm_b_re, ssm_b_im, ssm_c_re, ssm_c_im, ssm_d)
    y_s = jax.nn.gelu(y_s)
    br_s = (y_s @ w_s5_val) * jax.nn.sigmoid(y_s @ w_s5_gate)
    u_c = u_c2[..., :D_CONV] * jax.nn.sigmoid(u_c2[..., D_CONV:])
    v, new_buf = _causal_depthwise_conv(u_c, conv_buf, conv_w, conv_b)
    v = jax.nn.silu(_layer_norm(v, conv_ln_g, conv_ln_b))
    br_c = v @ w_conv_pw
    mix = (g_s * br_s + g_c * br_c) @ w_out
    x = _layer_norm(ALPHA * x + g1 * mix, ln1_g, ln1_b)

    h = _layer_norm(x) * (1.0 + sc2) + sh2
    ff = _hier_moe(h, moe_w_group, moe_b_group, moe_w_router, moe_b_router, moe_w_up, moe_w_gate, moe_w_down)
    x = _layer_norm(ALPHA * x + g2 * ff, ln2_g, ln2_b)
    return x, hT_re, hT_im, new_buf


def setup_inputs(seed: int = 0) -> dict:
    key = jax.random.key(seed)
    ks = iter(jax.random.split(key, 48))
    f32 = jnp.float32

    def nrm(shape, scale):
        return jax.random.normal(next(ks), shape, f32) * scale

    n_idx = jnp.arange(SSM_STATE, dtype=f32)
    inp = {}
    inp['x_prompt'] = nrm((BATCH, SEQ, D_MODEL), 1.0)
    inp['x_sample'] = nrm((DEC_BATCH, DEC_SEQ, D_MODEL), 1.0)
    inp['c_prompt'] = nrm((BATCH, D_MODEL), 1.0)
    inp['c_sample'] = nrm((DEC_BATCH, D_MODEL), 1.0)
    inp['state_ssm_re'] = nrm((DEPTH, DEC_BATCH, N_SSM_GROUPS, SSM_STATE), 0.3)
    inp['state_ssm_im'] = nrm((DEPTH, DEC_BATCH, N_SSM_GROUPS, SSM_STATE), 0.3)
    inp['state_conv'] = nrm((DEPTH, DEC_BATCH, CONV_WIDTH - 1, D_CONV), 0.5)
    inp['w_ada'] = nrm((DEPTH, D_MODEL, ADA_CHUNKS * D_MODEL), 0.5 * D_MODEL ** -0.5)
    inp['b_ada'] = nrm((DEPTH, ADA_CHUNKS * D_MODEL), 0.02)
    inp['w_in'] = nrm((DEPTH, D_MODEL, D_IN_PROJ), D_MODEL ** -0.5)
    inp['ssm_a_re'] = -0.5 + nrm((DEPTH, N_SSM_GROUPS, SSM_STATE), 0.01)
    inp['ssm_a_im'] = math.pi * n_idx + nrm((DEPTH, N_SSM_GROUPS, SSM_STATE), 0.01)
    inp['ssm_log_dt'] = jax.random.uniform(next(ks), (DEPTH, N_SSM_GROUPS), f32,
                                           minval=math.log(1e-3), maxval=math.log(1e-1))
    inp['ssm_b_re'] = nrm((DEPTH, N_SSM_GROUPS, SSM_STATE, SSM_GROUP), (2 * SSM_GROUP) ** -0.5)
    inp['ssm_b_im'] = nrm((DEPTH, N_SSM_GROUPS, SSM_STATE, SSM_GROUP), (2 * SSM_GROUP) ** -0.5)
    inp['ssm_c_re'] = nrm((DEPTH, N_SSM_GROUPS, SSM_GROUP, SSM_STATE), (2 * SSM_STATE) ** -0.5)
    inp['ssm_c_im'] = nrm((DEPTH, N_SSM_GROUPS, SSM_GROUP, SSM_STATE), (2 * SSM_STATE) ** -0.5)
    inp['ssm_d'] = nrm((DEPTH, D_SSM), 1.0)
    inp['w_s5_val'] = nrm((DEPTH, D_SSM, D_MODEL), D_SSM ** -0.5)
    inp['w_s5_gate'] = nrm((DEPTH, D_SSM, D_MODEL), D_SSM ** -0.5)
    inp['conv_w'] = nrm((DEPTH, CONV_WIDTH, D_CONV), CONV_WIDTH ** -0.5)
    inp['conv_b'] = nrm((DEPTH, D_CONV), 0.02)
    inp['conv_ln_g'] = 1.0 + nrm((DEPTH, D_CONV), 0.02)
    inp['conv_ln_b'] = nrm((DEPTH, D_CONV), 0.02)
    inp['w_conv_pw'] = nrm((DEPTH, D_CONV, D_MODEL), D_CONV ** -0.5)
    inp['w_out'] = nrm((DEPTH, D_MODEL, D_MODEL), BETA * D_MODEL ** -0.5)
    inp['ln1_g'] = 1.0 + nrm((DEPTH, D_MODEL), 0.02)
    inp['ln1_b'] = nrm((DEPTH, D_MODEL), 0.02)
    inp['moe_w_group'] = nrm((DEPTH, D_MODEL, N_EXPERT_GROUPS), D_MODEL ** -0.5)
    inp['moe_b_group'] = nrm((DEPTH, N_EXPERT_GROUPS), 0.01)
    inp['moe_w_router'] = nrm((DEPTH, D_MODEL, N_EXPERTS), D_MODEL ** -0.5)
    inp['moe_b_router'] = nrm((DEPTH, N_EXPERTS), 0.01)
    inp['moe_w_up'] = nrm((DEPTH, N_EXPERTS, D_MODEL, D_EXPERT), D_MODEL ** -0.5)
    inp['moe_w_gate'] = nrm((DEPTH, N_EXPERTS, D_MODEL, D_EXPERT), D_MODEL ** -0.5)
    inp['moe_w_down'] = nrm((DEPTH, N_EXPERTS, D_EXPERT, D_MODEL), BETA * D_EXPERT ** -0.5)
    inp['ln2_g'] = 1.0 + nrm((DEPTH, D_MODEL), 0.02)
    inp['ln2_b'] = nrm((DEPTH, D_MODEL), 0.02)
    return inp


def reference(x_prompt, x_sample, c_prompt, c_sample, state_ssm_re, state_ssm_im, state_conv,
              w_ada, b_ada, w_in, ssm_a_re, ssm_a_im, ssm_log_dt, ssm_b_re, ssm_b_im, ssm_c_re, ssm_c_im,
              ssm_d, w_s5_val, w_s5_gate, conv_w, conv_b, conv_ln_g, conv_ln_b, w_conv_pw, w_out,
              ln1_g, ln1_b, moe_w_group, moe_b_group, moe_w_router, moe_b_router,
              moe_w_up, moe_w_gate, moe_w_down, ln2_g, ln2_b):
    weights = (w_ada, b_ada, w_in, ssm_a_re, ssm_a_im, ssm_log_dt, ssm_b_re, ssm_b_im, ssm_c_re, ssm_c_im,
               ssm_d, w_s5_val, w_s5_gate, conv_w, conv_b, conv_ln_g, conv_ln_b, w_conv_pw, w_out,
               ln1_g, ln1_b, moe_w_group, moe_b_group, moe_w_router, moe_b_router,
               moe_w_up, moe_w_gate, moe_w_down, ln2_g, ln2_b)
    n_prompt = x_prompt.shape[0]
    zero_ssm = jnp.zeros((n_prompt, N_SSM_GROUPS, SSM_STATE), x_prompt.dtype)
    zero_conv = jnp.zeros((n_prompt, CONV_WIDTH - 1, D_CONV), x_prompt.dtype)

    xp, xs = x_prompt, x_sample
    p_re, p_im, p_conv, s_re, s_im, s_conv = [], [], [], [], [], []
    for l in range(DEPTH):
        lw = tuple(w[l] for w in weights)
        xp, hr, hi, cb = _layer(xp, c_prompt, zero_ssm, zero_ssm, zero_conv, *lw)
        p_re.append(hr); p_im.append(hi); p_conv.append(cb)
        xs, hr, hi, cb = _layer(xs, c_sample, state_ssm_re[l], state_ssm_im[l], state_conv[l], *lw)
        s_re.append(hr); s_im.append(hi); s_conv.append(cb)

    new_ssm_re_prompt = jnp.stack(p_re, axis=0)
    new_ssm_im_prompt = jnp.stack(p_im, axis=0)
    new_conv_prompt = jnp.stack(p_conv, axis=0)
    new_ssm_re_sample = jnp.stack(s_re, axis=0)
    new_ssm_im_sample = jnp.stack(s_im, axis=0)
    new_conv_sample = jnp.stack(s_conv, axis=0)
    return (xp, xs, new_ssm_re_prompt, new_ssm_im_prompt, new_conv_prompt,
            new_ssm_re_sample, new_ssm_im_sample, new_conv_sample)
```

```python
import functools
import math

import jax
import jax.numpy as jnp
from jax import lax
from jax.experimental import pallas as pl
from jax.experimental.pallas import tpu as pltpu

F32 = jnp.float32
BF16 = jnp.bfloat16

D_MODEL = 2048
DEPTH = 4
D_SSM = 1024
SSM_GROUP = 16
N_SSM_GROUPS = 64
SSM_STATE = 64
D_CONV = 1024
CONV_WIDTH = 31
N_EXPERT_GROUPS = 4
EXPERTS_PER_GROUP = 8
N_EXPERTS = 32
D_EXPERT = 256
ALPHA = (2 * DEPTH) ** 0.25
LN_EPS = 1e-5
ADA_CHUNKS = 6

LANES = 128
CHUNK = 16
GROUPS_PER_BLOCK = LANES // SSM_GROUP
N_GROUP_BLOCKS = N_SSM_GROUPS // GROUPS_PER_BLOCK
STATE_LANES = GROUPS_PER_BLOCK * SSM_STATE
VMEM_LIMIT = 56 * 1024 * 1024


def _cp(*sem):
    return pltpu.CompilerParams(dimension_semantics=sem, vmem_limit_bytes=VMEM_LIMIT)


def _ln(x):
    mu = jnp.mean(x, axis=-1, keepdims=True)
    xc = x - mu
    var = jnp.mean(xc * xc, axis=-1, keepdims=True)
    return xc * lax.rsqrt(var + LN_EPS)


def _sigmoid(x):
    return jax.nn.sigmoid(x)


def _dot(a, b):
    return jnp.dot(a, b, preferred_element_type=F32)


def _ada_kernel(c_ref, w_ref, b_ref, o_ref):
    c = c_ref[...]
    s = (c * _sigmoid(c)).astype(BF16)
    o_ref[0] = _dot(s, w_ref[0].astype(BF16)) + b_ref[0]


def _ada_call(c_all, w_ada, b_ada):
    rows = c_all.shape[0]
    n_out = ADA_CHUNKS * D_MODEL
    tn = 1024
    return pl.pallas_call(
        _ada_kernel,
        grid=(DEPTH, n_out // tn),
        in_specs=[
            pl.BlockSpec((rows, D_MODEL), lambda l, n: (0, 0)),
            pl.BlockSpec((1, D_MODEL, tn), lambda l, n: (l, 0, n)),
            pl.BlockSpec((1, 1, tn), lambda l, n: (l, 0, n)),
        ],
        out_specs=pl.BlockSpec((1, rows, tn), lambda l, n: (l, 0, n)),
        out_shape=jax.ShapeDtypeStruct((DEPTH, rows, n_out), F32),
        compiler_params=_cp("parallel", "parallel"),
        name="ada_proj",
    )(c_all, w_ada, b_ada.reshape(DEPTH, 1, n_out))


def _inproj_kernel(x_ref, sc_ref, sh_ref, wus_ref, wca_ref, wcb_ref, wg_ref,
                   us_ref, uc_ref, g_ref, h_scr):
    bs, ts, d = x_ref.shape

    @pl.when(pl.program_id(2) == 0)
    def _():
        h = _ln(x_ref[...]) * (1.0 + sc_ref[...]) + sh_ref[...]
        h_scr[...] = h.reshape(bs * ts, d).astype(BF16)

    h = h_scr[...]
    us_ref[...] = _dot(h, wus_ref[...])
    uc_ref[...] = _dot(h, wca_ref[...]) * _sigmoid(_dot(h, wcb_ref[...]))
    g_ref[...] = _sigmoid(_dot(h, wg_ref[...]))


def _inproj_call(x, sc, sh, w_us, w_ca, w_cb, w_g, bs, ts):
    nseq, s, d = x.shape
    ni, nj, nn = nseq // bs, s // ts, 4
    m = bs * ts
    t = nseq * s
    cn = D_SSM // nn
    gn = 2 * D_MODEL // nn
    row = lambda i, j, n: (i * nj + j, n)
    return pl.pallas_call(
        _inproj_kernel,
        grid=(ni, nj, nn),
        in_specs=[
            pl.BlockSpec((bs, ts, d), lambda i, j, n: (i, j, 0)),
            pl.BlockSpec((bs, 1, d), lambda i, j, n: (i, 0, 0)),
            pl.BlockSpec((bs, 1, d), lambda i, j, n: (i, 0, 0)),
            pl.BlockSpec((d, cn), lambda i, j, n: (0, n)),
            pl.BlockSpec((d, cn), lambda i, j, n: (0, n)),
            pl.BlockSpec((d, cn), lambda i, j, n: (0, n)),
            pl.BlockSpec((d, gn), lambda i, j, n: (0, n)),
        ],
        out_specs=[
            pl.BlockSpec((m, cn), row),
            pl.BlockSpec((m, cn), row),
            pl.BlockSpec((m, gn), row),
        ],
        out_shape=[
            jax.ShapeDtypeStruct((t, D_SSM), F32),
            jax.ShapeDtypeStruct((t, D_CONV), F32),
            jax.ShapeDtypeStruct((t, 2 * D_MODEL), F32),
        ],
        scratch_shapes=[pltpu.VMEM((m, d), BF16)],
        compiler_params=_cp("parallel", "parallel", "arbitrary"),
        name="in_proj",
    )(x, sc, sh, w_us, w_ca, w_cb, w_g)


def _cmul(ar, ai, br, bi):
    return ar * br - ai * bi, ar * bi + ai * br


def _discretize(a_re, a_im, log_dt):
    dt = jnp.exp(log_dt)
    mag = jnp.exp(dt * a_re)
    ang = dt * a_im
    ab_re = mag * jnp.cos(ang)
    ab_im = mag * jnp.sin(ang)
    den = a_re * a_re + a_im * a_im
    k_re = ((ab_re - 1.0) * a_re + ab_im * a_im) / den
    k_im = (ab_im * a_re - (ab_re - 1.0) * a_im) / den
    return ab_re, ab_im, k_re, k_im


def _s5prep_kernel(ar_ref, ai_ref, ldt_ref, a4r_ref, a4i_ref, ldt4_ref,
                   btr_ref, bti_ref, ctr_ref, cti_ref, cr_ref, ci_ref,
                   d_ref, bdz_ref, bdh_ref, ac_ref):
    ab_re, ab_im, k_re, k_im = _discretize(ar_ref[0, 0], ai_ref[0, 0], ldt_ref[0, 0])
    btr, bti = btr_ref[0, 0], bti_ref[0, 0]
    ctr, cti = ctr_ref[0, 0], cti_ref[0, 0]
    cr, ci = cr_ref[0, 0], ci_ref[0, 0]
    neg_cti = -cti

    pw_re = jnp.ones_like(ab_re)
    pw_im = jnp.zeros_like(ab_re)
    for m in range(CHUNK):
        w_re, w_im = _cmul(pw_re, pw_im, k_re, k_im)
        l_re = btr * w_re - bti * w_im
        l_im = btr * w_im + bti * w_re
        dm = (jnp.dot(l_re, ctr, preferred_element_type=F32, precision=lax.Precision.HIGHEST)
              + jnp.dot(l_im, neg_cti, preferred_element_type=F32, precision=lax.Precision.HIGHEST))
        d_ref[0, 0, m] = dm.astype(BF16)
        tp = CHUNK - 1 - m
        bdz_ref[0, 0, tp * LANES:(tp + 1) * LANES, 0:STATE_LANES] = l_re.astype(BF16)
        bdz_ref[0, 0, tp * LANES:(tp + 1) * LANES, STATE_LANES:2 * STATE_LANES] = l_im.astype(BF16)
        pw_re, pw_im = _cmul(pw_re, pw_im, ab_re, ab_im)
        bdh_ref[0, 0, m * LANES:(m + 1) * LANES, 0:STATE_LANES] = (cr * pw_re - ci * pw_im).astype(BF16)
        bdh_ref[0, 0, m * LANES:(m + 1) * LANES, STATE_LANES:2 * STATE_LANES] = (
            -(cr * pw_im + ci * pw_re)).astype(BF16)

    q_re, q_im, _, _ = _discretize(a4r_ref[0, 0], a4i_ref[0, 0], ldt4_ref[0, 0])
    for _ in range(3):
        q_re, q_im = _cmul(q_re, q_im, q_re, q_im)
    ac_ref[0, 0, 8:12, :] = q_re
    ac_ref[0, 0, 12:16, :] = q_im
    q_re, q_im = _cmul(q_re, q_im, q_re, q_im)
    ac_ref[0, 0, 0:4, :] = q_re
    ac_ref[0, 0, 4:8, :] = q_im


def _s5prep_call(ssm_a_re, ssm_a_im, ssm_log_dt, ssm_b_re, ssm_b_im, ssm_c_re, ssm_c_im):
    nb, gb = N_GROUP_BLOCKS, GROUPS_PER_BLOCK
    eye = jnp.eye(gb, dtype=F32)
    row = lambda a: a.reshape(DEPTH, nb, 1, STATE_LANES)
    quad = lambda a: a.reshape(DEPTH, nb, 4, LANES)
    ldt = jnp.broadcast_to(ssm_log_dt[:, :, None], (DEPTH, N_SSM_GROUPS, SSM_STATE))

    def bt_bd(b):
        bt = b.reshape(DEPTH, nb, gb, SSM_STATE, SSM_GROUP).transpose(0, 1, 2, 4, 3)
        return jnp.einsum('lngkp,gh->lngkhp', bt, eye).reshape(DEPTH, nb, LANES, STATE_LANES)

    def ct_bd(c):
        ct = c.reshape(DEPTH, nb, gb, SSM_GROUP, SSM_STATE).transpose(0, 1, 2, 4, 3)
        return jnp.einsum('lngpk,gh->lngphk', ct, eye).reshape(DEPTH, nb, STATE_LANES, LANES)

    def c_bd(c):
        cc = c.reshape(DEPTH, nb, gb, SSM_GROUP, SSM_STATE)
        return jnp.einsum('lngkp,gh->lngkhp', cc, eye).reshape(DEPTH, nb, LANES, STATE_LANES)

    blk = lambda *shape: pl.BlockSpec((1, 1) + shape, lambda l, g: (l, g) + (0,) * len(shape))
    return pl.pallas_call(
        _s5prep_kernel,
        grid=(DEPTH, nb),
        in_specs=[blk(1, STATE_LANES)] * 3 + [blk(4, LANES)] * 3
        + [blk(LANES, STATE_LANES)] * 2 + [blk(STATE_LANES, LANES)] * 2 + [blk(LANES, STATE_LANES)] * 2,
        out_specs=[blk(CHUNK, LANES, LANES), blk(CHUNK * LANES, 2 * STATE_LANES),
                   blk(CHUNK * LANES, 2 * STATE_LANES), blk(16, LANES)],
        out_shape=[
            jax.ShapeDtypeStruct((DEPTH, nb, CHUNK, LANES, LANES), BF16),
            jax.ShapeDtypeStruct((DEPTH, nb, CHUNK * LANES, 2 * STATE_LANES), BF16),
            jax.ShapeDtypeStruct((DEPTH, nb, CHUNK * LANES, 2 * STATE_LANES), BF16),
            jax.ShapeDtypeStruct((DEPTH, nb, 16, LANES), F32),
        ],
        compiler_params=_cp("parallel", "parallel"),
        name="s5_prep",
    )(row(ssm_a_re), row(ssm_a_im), row(ldt), quad(ssm_a_re), quad(ssm_a_im), quad(ldt),
      bt_bd(ssm_b_re), bt_bd(ssm_b_im), ct_bd(ssm_c_re), ct_bd(ssm_c_im), c_bd(ssm_c_re), c_bd(ssm_c_im))


def _gelu_tanh(x):
    return x * (0.5 * (1.0 + jnp.tanh(math.sqrt(2.0 / math.pi) * (x + 0.044715 * (x * x * x)))))


def _s5_kernel(us_ref, h0_ref, d_ref, bdz_ref, bdh_ref, ac_ref, dsk_ref,
               y_ref, ht_ref, bd_scr, ucat_scr, y_scr, zs_scr, hs_scr, hcat_scr, *, lc, nsq, nc):
    m = nsq * nc
    kd = lc * LANES
    first_b = pl.program_id(1) == 0

    @pl.when(jnp.logical_and(pl.program_id(0) == 0, first_b))
    def _():
        bd_scr[...] = jnp.zeros_like(bd_scr)

    @pl.when(first_b)
    def _():
        for tp in range(lc):
            for t in range(tp, lc):
                bd_scr[tp * LANES:(tp + 1) * LANES, t * LANES:(t + 1) * LANES] = d_ref[0, 0, t - tp]

    for t in range(lc):
        ucat_scr[:, t * LANES:(t + 1) * LANES] = us_ref[pl.ds(t, m, stride=lc), :].astype(BF16)
    ucat = ucat_scr[...]
    y_scr[...] = _dot(ucat, bd_scr[0:kd, 0:kd])
    z = _dot(ucat, bdz_ref[0, 0, (CHUNK - lc) * LANES:CHUNK * LANES, :])
    for s in range(8):
        zs_scr[:, s, :] = z[:, s * LANES:(s + 1) * LANES]

    r0 = 0 if lc == CHUNK else 8
    a_re = ac_ref[0, 0, r0:r0 + 4, :]
    a_im = ac_ref[0, 0, r0 + 4:r0 + 8, :]
    if nc == 1:
        h = h0_ref[:, 0]
        hs_scr[...] = h
        h_re, h_im = h[:, 0:4, :], h[:, 4:8, :]
        zz = zs_scr[...]
        ht_ref[:, 0, 0:4, :] = a_re * h_re - a_im * h_im + zz[:, 0:4, :]
        ht_ref[:, 0, 4:8, :] = a_re * h_im + a_im * h_re + zz[:, 4:8, :]
    else:
        def step(c, carry):
            h_re, h_im = carry
            hs_scr[c, 0:4, :] = h_re
            hs_scr[c, 4:8, :] = h_im
            zz = zs_scr[c]
            return (a_re * h_re - a_im * h_im + zz[0:4, :], a_re * h_im + a_im * h_re + zz[4:8, :])

        h_re, h_im = lax.fori_loop(0, nc, step, (h0_ref[0, 0, 0:4, :], h0_ref[0, 0, 4:8, :]))
        ht_ref[0, 0, 0:4, :] = h_re
        ht_ref[0, 0, 4:8, :] = h_im

    for s in range(8):
        hcat_scr[:, s * LANES:(s + 1) * LANES] = hs_scr[:, s, :].astype(BF16)
    y = y_scr[...] + lax.dot_general(hcat_scr[...], bdh_ref[0, 0, 0:kd, :], (((1,), (1,)), ((), ())),
                                     preferred_element_type=F32)
    for t in range(lc):
        u = us_ref[pl.ds(t, m, stride=lc), :]
        v = y[:, t * LANES:(t + 1) * LANES] + dsk_ref[...] * u
        y_ref[pl.ds(t, m, stride=lc), :] = _gelu_tanh(v).astype(y_ref.dtype)


def _s5_call(us, h0, dmat, bdz, bdh, acoef, dskip, layer, nseq, s):
    if s % CHUNK == 0:
        lc, nsq, nc = CHUNK, 1, s // CHUNK
    else:
        lc, nsq, nc = s, nseq, 1
    nb_seq = nseq // nsq
    m = nsq * nc
    rows = m * lc
    lay = lambda *shape: pl.BlockSpec((1, 1) + shape, lambda g, b: (layer, g) + (0,) * len(shape))
    kern = functools.partial(_s5_kernel, lc=lc, nsq=nsq, nc=nc)
    return pl.pallas_call(
        kern,
        grid=(N_GROUP_BLOCKS, nb_seq),
        in_specs=[
            pl.BlockSpec((rows, LANES), lambda g, b: (b, g)),
            pl.BlockSpec((nsq, 1, 8, LANES), lambda g, b: (b, g, 0, 0)),
            lay(CHUNK, LANES, LANES),
            lay(CHUNK * LANES, 2 * STATE_LANES),
            lay(CHUNK * LANES, 2 * STATE_LANES),
            lay(16, LANES),
            pl.BlockSpec((1, LANES), lambda g, b: (0, g)),
        ],
        out_specs=[
            pl.BlockSpec((rows, LANES), lambda g, b: (b, g)),
            pl.BlockSpec((nsq, 1, 8, LANES), lambda g, b: (b, g, 0, 0)),
        ],
        out_shape=[
            jax.ShapeDtypeStruct((nseq * s, D_SSM), F32),
            jax.ShapeDtypeStruct((nseq, N_GROUP_BLOCKS, 8, LANES), F32),
        ],
        scratch_shapes=[
            pltpu.VMEM((CHUNK * LANES, CHUNK * LANES), BF16),
            pltpu.VMEM((m, lc * LANES), BF16),
            pltpu.VMEM((m, lc * LANES), F32),
            pltpu.VMEM((m, 8, LANES), F32),
            pltpu.VMEM((m, 8, LANES), F32),
            pltpu.VMEM((m, 2 * STATE_LANES), BF16),
        ],
        compiler_params=_cp("arbitrary", "arbitrary"),
        name="s5_scan",
    )(us, h0, dmat, bdz, bdh, acoef, dskip)


def _state_to_blocks(h_re, h_im):
    nseq = h_re.shape[0]
    q = lambda a: a.reshape(nseq, N_GROUP_BLOCKS, 4, LANES)
    return jnp.concatenate([q(h_re), q(h_im)], axis=2)


def _blocks_to_state(hb):
    nseq = hb.shape[0]
    return (hb[:, :, 0:4, :].reshape(nseq, N_SSM_GROUPS, SSM_STATE),
            hb[:, :, 4:8, :].reshape(nseq, N_SSM_GROUPS, SSM_STATE))


HALO = CONV_WIDTH - 1
PAD0 = 32 - HALO


def _conv_kernel(u_ref, st_ref, w_ref, b_ref, g_ref, beta_ref, v_ref, nst_ref, xp_scr, cv_scr, *, rr):
    bs, ts, c = u_ref.shape
    first = pl.program_id(1) == 0
    for b in range(bs):
        @pl.when(first)
        def _():
            xp_scr[b, PAD0:32, :] = st_ref[b]

        xp_scr[b, 32:32 + ts, :] = u_ref[b]

        def body(r, carry):
            base = pl.multiple_of(r * rr, rr)
            for c0 in range(0, c, LANES):
                win = xp_scr[b, pl.ds(base, rr + 32), c0:c0 + LANES]
                acc = jnp.zeros((rr, LANES), F32)
                for ph in range(8):
                    offs = [PAD0 + k for k in range(CONV_WIDTH) if (PAD0 + k) % 8 == ph]
                    span = max(offs) - ph + rr
                    shifted = win[ph:ph + span]
                    for o in offs:
                        a = o - ph
                        acc = acc + w_ref[o - PAD0:o - PAD0 + 1, c0:c0 + LANES] * shifted[a:a + rr]
                cv_scr[:, c0:c0 + LANES] = acc
            y = _ln(cv_scr[...] + b_ref[...]) * g_ref[...] + beta_ref[...]
            v_ref[b, pl.ds(base, rr), :] = (y * _sigmoid(y)).astype(v_ref.dtype)
            return carry

        lax.fori_loop(0, ts // rr, body, 0)
        tail = xp_scr[b, ts + PAD0:ts + 32, :]
        nst_ref[b] = tail
        xp_scr[b, PAD0:32, :] = tail


def _conv_call(uc, state, conv_w, conv_b, ln_g, ln_b, bs, ts):
    nseq, s, c = uc.shape
    rr = min(ts, 64)
    vec = lambda a: a.reshape(1, c)
    cst = lambda shape: pl.BlockSpec(shape, lambda i, j: (0,) * len(shape))
    return pl.pallas_call(
        functools.partial(_conv_kernel, rr=rr),
        grid=(nseq // bs, s // ts),
        in_specs=[
            pl.BlockSpec((bs, ts, c), lambda i, j: (i, j, 0)),
            pl.BlockSpec((bs, HALO, c), lambda i, j: (i, 0, 0)),
            cst((CONV_WIDTH, c)), cst((1, c)), cst((1, c)), cst((1, c)),
        ],
        out_specs=[
            pl.BlockSpec((bs, ts, c), lambda i, j: (i, j, 0)),
            pl.BlockSpec((bs, HALO, c), lambda i, j: (i, 0, 0)),
        ],
        out_shape=[
            jax.ShapeDtypeStruct((nseq, s, c), F32),
            jax.ShapeDtypeStruct((nseq, HALO, c), F32),
        ],
        scratch_shapes=[pltpu.VMEM((bs, ts + 32, c), F32), pltpu.VMEM((rr, c), F32)],
        compiler_params=_cp("parallel", "arbitrary"),
        name="conv_module",
    )(uc, state, conv_w, vec(conv_b), vec(ln_g), vec(ln_b))


def _route(logits):
    lane = lax.broadcasted_iota(jnp.int32, logits.shape, 1).astype(F32)
    neg = jnp.float32(-jnp.inf)
    big = jnp.float32(1e9)
    gmask = jnp.logical_and(lane >= N_EXPERTS, lane < N_EXPERTS + N_EXPERT_GROUPS)
    gmax = jnp.max(jnp.where(gmask, logits, neg), axis=-1, keepdims=True)
    gsel = jnp.min(jnp.where(jnp.logical_and(gmask, logits == gmax), lane, big), axis=-1, keepdims=True)
    p_g = 1.0 / jnp.sum(jnp.where(gmask, jnp.exp(logits - gmax), 0.0), axis=-1, keepdims=True)
    lo = (gsel - N_EXPERTS) * EXPERTS_PER_GROUP
    emask = jnp.logical_and(lane >= lo, lane < lo + EXPERTS_PER_GROUP)
    v1 = jnp.max(jnp.where(emask, logits, neg), axis=-1, keepdims=True)
    i1 = jnp.min(jnp.where(jnp.logical_and(emask, logits == v1), lane, big), axis=-1, keepdims=True)
    emask2 = jnp.logical_and(emask, lane != i1)
    v2 = jnp.max(jnp.where(emask2, logits, neg), axis=-1, keepdims=True)
    i2 = jnp.min(jnp.where(jnp.logical_and(emask2, logits == v2), lane, big), axis=-1, keepdims=True)
    e2 = jnp.exp(v2 - v1)
    den = 1.0 + e2
    w1 = (1.0 / den) * p_g
    w2 = (e2 / den) * p_g
    return jnp.where(lane == i1, w1, 0.0) + jnp.where(lane == i2, w2, 0.0)


def _mix_kernel(ys_ref, v_ref, gs_ref, gc_ref, x_ref, g1_ref, sc2_ref, sh2_ref,
                wv_ref, wg_ref, wpw_ref, wo_ref, l1g_ref, l1b_ref, wr_ref, br_ref,
                x1_ref, h2_ref, comb_ref, acc_scr):
    n = pl.program_id(2)
    bs, ts, d = x_ref.shape
    ys = ys_ref[...].astype(BF16)
    br_s = _dot(ys, wv_ref[...]) * _sigmoid(_dot(ys, wg_ref[...]))
    br_c = _dot(v_ref[...].astype(BF16), wpw_ref[...])
    mix = gs_ref[...] * br_s + gc_ref[...] * br_c
    contrib = _dot(mix.astype(BF16), wo_ref[...])

    @pl.when(n == 0)
    def _():
        acc_scr[...] = contrib

    @pl.when(n > 0)
    def _():
        acc_scr[...] += contrib

    @pl.when(n == pl.num_programs(2) - 1)
    def _():
        o = acc_scr[...].reshape(bs, ts, d)
        x1 = _ln(ALPHA * x_ref[...] + g1_ref[...] * o) * l1g_ref[...] + l1b_ref[...]
        x1_ref[...] = x1
        h2 = (_ln(x1) * (1.0 + sc2_ref[...]) + sh2_ref[...]).reshape(bs * ts, d)
        h2_ref[...] = h2.astype(BF16)
        logits = jnp.dot(h2, wr_ref[...], preferred_element_type=F32,
                         precision=lax.Precision.HIGHEST) + br_ref[...]
        comb_ref[...] = _route(logits)


def _mix_call(ys, v, g, x, g1, sc2, sh2, w_val, w_gate, w_pw, w_out, ln_g, ln_b, w_r, b_r, bs, ts):
    nseq, s, d = x.shape
    ni, nj, nn = nseq // bs, s // ts, 4
    m = bs * ts
    t = nseq * s
    cn = d // nn
    row0 = lambda i, j, n: (i * nj + j, 0)
    mod = pl.BlockSpec((bs, 1, d), lambda i, j, n: (i, 0, 0))
    vec = pl.BlockSpec((1, d), lambda i, j, n: (0, 0))
    return pl.pallas_call(
        _mix_kernel,
        grid=(ni, nj, nn),
        in_specs=[
            pl.BlockSpec((m, D_SSM), row0),
            pl.BlockSpec((m, D_CONV), row0),
            pl.BlockSpec((m, cn), lambda i, j, n: (i * nj + j, n)),
            pl.BlockSpec((m, cn), lambda i, j, n: (i * nj + j, nn + n)),
            pl.BlockSpec((bs, ts, d), lambda i, j, n: (i, j, 0)),
            mod, mod, mod,
            pl.BlockSpec((D_SSM, cn), lambda i, j, n: (0, n)),
            pl.BlockSpec((D_SSM, cn), lambda i, j, n: (0, n)),
            pl.BlockSpec((D_CONV, cn), lambda i, j, n: (0, n)),
            pl.BlockSpec((cn, d), lambda i, j, n: (n, 0)),
            vec, vec,
            pl.BlockSpec((d, LANES), lambda i, j, n: (0, 0)),
            pl.BlockSpec((1, LANES), lambda i, j, n: (0, 0)),
        ],
        out_specs=[
            pl.BlockSpec((bs, ts, d), lambda i, j, n: (i, j, 0)),
            pl.BlockSpec((m, d), row0),
            pl.BlockSpec((m, LANES), row0),
        ],
        out_shape=[
            jax.ShapeDtypeStruct((nseq, s, d), F32),
            jax.ShapeDtypeStruct((t, d), BF16),
            jax.ShapeDtypeStruct((t, LANES), F32),
        ],
        scratch_shapes=[pltpu.VMEM((m, d), F32)],
        compiler_params=_cp("parallel", "parallel", "arbitrary"),
        name="branch_mix",
    )(ys, v, g, g, x, g1, sc2, sh2, w_val, w_gate, w_pw, w_out,
      ln_g.reshape(1, d), ln_b.reshape(1, d), w_r, b_r)


def _moe_kernel(h2_ref, comb_ref, wu_ref, wg_ref, wd_ref, x1_ref, g2_ref, l2g_ref, l2b_ref,
                x2_ref, acc_scr):
    e = pl.program_id(2)
    bs, ts, d = x1_ref.shape
    h = h2_ref[...]
    up = _dot(h, wu_ref[0])
    gt = _dot(h, wg_ref[0])
    comb = comb_ref[...]
    lane = lax.broadcasted_iota(jnp.int32, comb.shape, 1)
    ce = jnp.sum(jnp.where(lane == e, comb, 0.0), axis=-1, keepdims=True)
    act = (gt * _sigmoid(gt)) * up * ce
    contrib = _dot(act.astype(BF16), wd_ref[0])

    @pl.when(e == 0)
    def _():
        acc_scr[...] = contrib

    @pl.when(e > 0)
    def _():
        acc_scr[...] += contrib

    @pl.when(e == pl.num_programs(2) - 1)
    def _():
        ff = acc_scr[...].reshape(bs, ts, d)
        x2_ref[...] = _ln(ALPHA * x1_ref[...] + g2_ref[...] * ff) * l2g_ref[...] + l2b_ref[...]


def _moe_call(h2, comb, w_up, w_gate, w_down, x1, g2, ln_g, ln_b, bs, ts):
    nseq, s, d = x1.shape
    ni, nj = nseq // bs, s // ts
    m = bs * ts
    row0 = lambda i, j, e: (i * nj + j, 0)
    vec = pl.BlockSpec((1, d), lambda i, j, e: (0, 0))
    return pl.pallas_call(
        _moe_kernel,
        grid=(ni, nj, N_EXPERTS),
        in_specs=[
            pl.BlockSpec((m, d), row0),
            pl.BlockSpec((m, LANES), row0),
            pl.BlockSpec((1, d, D_EXPERT), lambda i, j, e: (e, 0, 0)),
            pl.BlockSpec((1, d, D_EXPERT), lambda i, j, e: (e, 0, 0)),
            pl.BlockSpec((1, D_EXPERT, d), lambda i, j, e: (e, 0, 0)),
            pl.BlockSpec((bs, ts, d), lambda i, j, e: (i, j, 0)),
            pl.BlockSpec((bs, 1, d), lambda i, j, e: (i, 0, 0)),
            vec, vec,
        ],
        out_specs=pl.BlockSpec((bs, ts, d), lambda i, j, e: (i, j, 0)),
        out_shape=jax.ShapeDtypeStruct((nseq, s, d), F32),
        scratch_shapes=[pltpu.VMEM((m, d), F32)],
        compiler_params=_cp("parallel", "parallel", "arbitrary"),
        name="moe_experts",
    )(h2, comb, w_up, w_gate, w_down, x1, g2, ln_g.reshape(1, d), ln_b.reshape(1, d))


def kernel(x_prompt, x_sample, c_prompt, c_sample, state_ssm_re, state_ssm_im, state_conv, w_ada, b_ada, w_in, ssm_a_re, ssm_a_im, ssm_log_dt, ssm_b_re, ssm_b_im, ssm_c_re, ssm_c_im, ssm_d, w_s5_val, w_s5_gate, conv_w, conv_b, conv_ln_g, conv_ln_b, w_conv_pw, w_out, ln1_g, ln1_b, moe_w_group, moe_b_group, moe_w_router, moe_b_router, moe_w_up, moe_w_gate, moe_w_down, ln2_g, ln2_b):
    n_p, s_p = x_prompt.shape[0], x_prompt.shape[1]
    n_s, s_s = x_sample.shape[0], x_sample.shape[1]

    pad_rows = (-(n_p + n_s)) % 8
    c_all = jnp.concatenate([c_prompt, c_sample, jnp.zeros((pad_rows, D_MODEL), F32)], axis=0)
    ada = _ada_call(c_all, w_ada, b_ada)
    dmat, bdz, bdh, acoef = _s5prep_call(ssm_a_re, ssm_a_im, ssm_log_dt, ssm_b_re, ssm_b_im,
                                         ssm_c_re, ssm_c_im)

    w_us = w_in[:, :, :D_SSM].astype(BF16)
    w_ca = w_in[:, :, D_SSM:D_SSM + D_CONV].astype(BF16)
    w_cb = w_in[:, :, D_SSM + D_CONV:D_SSM + 2 * D_CONV].astype(BF16)
    w_g = w_in[:, :, D_SSM + 2 * D_CONV:].astype(BF16)
    w_val = w_s5_val.astype(BF16)
    w_gate = w_s5_gate.astype(BF16)
    w_pw = w_conv_pw.astype(BF16)
    w_o = w_out.astype(BF16)
    w_up = moe_w_up.astype(BF16)
    w_gt = moe_w_gate.astype(BF16)
    w_dn = moe_w_down.astype(BF16)
    zpad = LANES - N_EXPERTS - N_EXPERT_GROUPS
    w_r = jnp.concatenate([moe_w_router, moe_w_group, jnp.zeros((DEPTH, D_MODEL, zpad), F32)], axis=-1)
    b_r = jnp.concatenate([moe_b_router, moe_b_group, jnp.zeros((DEPTH, zpad), F32)], axis=-1)
    b_r = b_r.reshape(DEPTH, 1, LANES)
    dskip = ssm_d.reshape(DEPTH, 1, D_SSM)

    streams = [
        dict(x=x_prompt, rows=slice(0, n_p), n=n_p, s=s_p, bs=1, ts=512, mbs=1, mts=256,
             h0=jnp.zeros((n_p, N_GROUP_BLOCKS, 8, LANES), F32),
             conv=jnp.zeros((n_p, HALO, D_CONV), F32), cbs=1, cts=512, per_layer_state=False),
        dict(x=x_sample, rows=slice(n_p, n_p + n_s), n=n_s, s=s_s, bs=64, ts=s_s, mbs=32, mts=s_s,
             cbs=8, cts=s_s, per_layer_state=True),
    ]
    outs = [dict(re=[], im=[], conv=[]) for _ in streams]

    for l in range(DEPTH):
        for st, out in zip(streams, outs):
            n, s, bs, ts = st['n'], st['s'], st['bs'], st['ts']
            x = st['x']
            mod = lambda k: ada[l, st['rows'], k * D_MODEL:(k + 1) * D_MODEL].reshape(n, 1, D_MODEL)
            sh1, sc1, g1, sh2, sc2, g2 = (mod(k) for k in range(ADA_CHUNKS))
            if st['per_layer_state']:
                h0 = _state_to_blocks(state_ssm_re[l], state_ssm_im[l])
                cst = state_conv[l]
            else:
                h0, cst = st['h0'], st['conv']

            us, uc, g = _inproj_call(x, sc1, sh1, w_us[l], w_ca[l], w_cb[l], w_g[l], bs, ts)
            ys, ht = _s5_call(us, h0, dmat, bdz, bdh, acoef, dskip[l], l, n, s)
            v, nbuf = _conv_call(uc.reshape(n, s, D_CONV), cst, conv_w[l], conv_b[l],
                                 conv_ln_g[l], conv_ln_b[l], st['cbs'], st['cts'])
            x1, h2, comb = _mix_call(ys, v.reshape(n * s, D_CONV), g, x, g1, sc2, sh2,
                                     w_val[l], w_gate[l], w_pw[l], w_o[l], ln1_g[l], ln1_b[l],
                                     w_r[l], b_r[l], st['mbs'], st['mts'])
            x2 = _moe_call(h2, comb, w_up[l], w_gt[l], w_dn[l], x1, g2, ln2_g[l], ln2_b[l], bs, ts)
            st['x'] = x2
            h_re, h_im = _blocks_to_state(ht)
            out['re'].append(h_re)
            out['im'].append(h_im)
            out['conv'].append(nbuf)

    stack = lambda xs: jnp.stack(xs, axis=0)
    po, so = outs
    return (streams[0]['x'], streams[1]['x'],
            stack(po['re']), stack(po['im']), stack(po['conv']),
            stack(so['re']), stack(so['im']), stack(so['conv']))
```

```python
import functools
import math

import jax
import jax.numpy as jnp
from jax import lax
from jax.experimental import pallas as pl
from jax.experimental.pallas import tpu as pltpu

F32 = jnp.float32
BF16 = jnp.bfloat16

D_MODEL = 2048
DEPTH = 4
D_SSM = 1024
SSM_GROUP = 16
N_SSM_GROUPS = 64
SSM_STATE = 64
D_CONV = 1024
CONV_WIDTH = 31
N_EXPERT_GROUPS = 4
EXPERTS_PER_GROUP = 8
N_EXPERTS = 32
D_EXPERT = 256
ALPHA = (2 * DEPTH) ** 0.25
LN_EPS = 1e-5
ADA_CHUNKS = 6

LANES = 128
CHUNK = 16
GROUPS_PER_BLOCK = LANES // SSM_GROUP
N_GROUP_BLOCKS = N_SSM_GROUPS // GROUPS_PER_BLOCK
STATE_LANES = GROUPS_PER_BLOCK * SSM_STATE
VMEM_LIMIT = 56 * 1024 * 1024
ROUTE_GROUP_LANE = N_EXPERTS
MOE_TILE = 512
GATHER_ROWS = 512


def _cp(*sem):
    return pltpu.CompilerParams(dimension_semantics=sem, vmem_limit_bytes=VMEM_LIMIT)


def _ln(x):
    mu = jnp.mean(x, axis=-1, keepdims=True)
    xc = x - mu
    var = jnp.mean(xc * xc, axis=-1, keepdims=True)
    return xc * lax.rsqrt(var + LN_EPS)


def _sigmoid(x):
    return jax.nn.sigmoid(x)


def _dot(a, b):
    return jnp.dot(a, b, preferred_element_type=F32)


def _ada_kernel(c_ref, w_ref, b_ref, o_ref):
    c = c_ref[...]
    s = (c * _sigmoid(c)).astype(BF16)
    o_ref[0] = _dot(s, w_ref[0].astype(BF16)) + b_ref[0]


def _ada_call(c_all, w_ada, b_ada):
    rows = c_all.shape[0]
    n_out = ADA_CHUNKS * D_MODEL
    tn = 1024
    return pl.pallas_call(
        _ada_kernel,
        grid=(DEPTH, n_out // tn),
        in_specs=[
            pl.BlockSpec((rows, D_MODEL), lambda l, n: (0, 0)),
            pl.BlockSpec((1, D_MODEL, tn), lambda l, n: (l, 0, n)),
            pl.BlockSpec((1, 1, tn), lambda l, n: (l, 0, n)),
        ],
        out_specs=pl.BlockSpec((1, rows, tn), lambda l, n: (l, 0, n)),
        out_shape=jax.ShapeDtypeStruct((DEPTH, rows, n_out), F32),
        compiler_params=_cp("parallel", "parallel"),
        name="ada_proj",
    )(c_all, w_ada, b_ada.reshape(DEPTH, 1, n_out))


def _inproj_kernel(x_ref, sc_ref, sh_ref, wus_ref, wca_ref, wcb_ref, wg_ref,
                   us_ref, uc_ref, g_ref, h_scr):
    bs, ts, d = x_ref.shape

    @pl.when(pl.program_id(2) == 0)
    def _():
        h = _ln(x_ref[...]) * (1.0 + sc_ref[...]) + sh_ref[...]
        h_scr[...] = h.reshape(bs * ts, d).astype(BF16)

    h = h_scr[...]
    us_ref[...] = _dot(h, wus_ref[...])
    uc_ref[...] = _dot(h, wca_ref[...]) * _sigmoid(_dot(h, wcb_ref[...]))
    g_ref[...] = _sigmoid(_dot(h, wg_ref[...])).astype(g_ref.dtype)


def _inproj_call(x, sc, sh, w_us, w_ca, w_cb, w_g, bs, ts):
    nseq, s, d = x.shape
    ni, nj, nn = nseq // bs, s // ts, 4
    m = bs * ts
    t = nseq * s
    cn = D_SSM // nn
    gn = 2 * D_MODEL // nn
    row = lambda i, j, n: (i * nj + j, n)
    return pl.pallas_call(
        _inproj_kernel,
        grid=(ni, nj, nn),
        in_specs=[
            pl.BlockSpec((bs, ts, d), lambda i, j, n: (i, j, 0)),
            pl.BlockSpec((bs, 1, d), lambda i, j, n: (i, 0, 0)),
            pl.BlockSpec((bs, 1, d), lambda i, j, n: (i, 0, 0)),
            pl.BlockSpec((d, cn), lambda i, j, n: (0, n)),
            pl.BlockSpec((d, cn), lambda i, j, n: (0, n)),
            pl.BlockSpec((d, cn), lambda i, j, n: (0, n)),
            pl.BlockSpec((d, gn), lambda i, j, n: (0, n)),
        ],
        out_specs=[
            pl.BlockSpec((m, cn), row),
            pl.BlockSpec((m, cn), row),
            pl.BlockSpec((m, gn), row),
        ],
        out_shape=[
            jax.ShapeDtypeStruct((t, D_SSM), F32),
            jax.ShapeDtypeStruct((t, D_CONV), F32),
            jax.ShapeDtypeStruct((t, 2 * D_MODEL), BF16),
        ],
        scratch_shapes=[pltpu.VMEM((m, d), BF16)],
        compiler_params=_cp("parallel", "parallel", "arbitrary"),
        name="in_proj",
    )(x, sc, sh, w_us, w_ca, w_cb, w_g)


def _cmul(ar, ai, br, bi):
    return ar * br - ai * bi, ar * bi + ai * br


def _discretize(a_re, a_im, log_dt):
    dt = jnp.exp(log_dt)
    mag = jnp.exp(dt * a_re)
    ang = dt * a_im
    ab_re = mag * jnp.cos(ang)
    ab_im = mag * jnp.sin(ang)
    den = a_re * a_re + a_im * a_im
    k_re = ((ab_re - 1.0) * a_re + ab_im * a_im) / den
    k_im = (ab_im * a_re - (ab_re - 1.0) * a_im) / den
    return ab_re, ab_im, k_re, k_im


def _s5prep_kernel(ar_ref, ai_ref, ldt_ref, a4r_ref, a4i_ref, ldt4_ref,
                   btr_ref, bti_ref, ctr_ref, cti_ref, cr_ref, ci_ref,
                   d_ref, bdz_ref, bdh_ref, ac_ref):
    ab_re, ab_im, k_re, k_im = _discretize(ar_ref[0, 0], ai_ref[0, 0], ldt_ref[0, 0])
    btr, bti = btr_ref[0, 0], bti_ref[0, 0]
    ctr, cti = ctr_ref[0, 0], cti_ref[0, 0]
    cr, ci = cr_ref[0, 0], ci_ref[0, 0]
    neg_cti = -cti

    pw_re = jnp.ones_like(ab_re)
    pw_im = jnp.zeros_like(ab_re)
    for m in range(CHUNK):
        w_re, w_im = _cmul(pw_re, pw_im, k_re, k_im)
        l_re = btr * w_re - bti * w_im
        l_im = btr * w_im + bti * w_re
        dm = (jnp.dot(l_re, ctr, preferred_element_type=F32, precision=lax.Precision.HIGHEST)
              + jnp.dot(l_im, neg_cti, preferred_element_type=F32, precision=lax.Precision.HIGHEST))
        d_ref[0, 0, m] = dm.astype(BF16)
        tp = CHUNK - 1 - m
        bdz_ref[0, 0, tp * LANES:(tp + 1) * LANES, 0:STATE_LANES] = l_re.astype(BF16)
        bdz_ref[0, 0, tp * LANES:(tp + 1) * LANES, STATE_LANES:2 * STATE_LANES] = l_im.astype(BF16)
        pw_re, pw_im = _cmul(pw_re, pw_im, ab_re, ab_im)
        bdh_ref[0, 0, m * LANES:(m + 1) * LANES, 0:STATE_LANES] = (cr * pw_re - ci * pw_im).astype(BF16)
        bdh_ref[0, 0, m * LANES:(m + 1) * LANES, STATE_LANES:2 * STATE_LANES] = (
            -(cr * pw_im + ci * pw_re)).astype(BF16)

    q_re, q_im, _, _ = _discretize(a4r_ref[0, 0], a4i_ref[0, 0], ldt4_ref[0, 0])
    for _ in range(3):
        q_re, q_im = _cmul(q_re, q_im, q_re, q_im)
    ac_ref[0, 0, 8:12, :] = q_re
    ac_ref[0, 0, 12:16, :] = q_im
    q_re, q_im = _cmul(q_re, q_im, q_re, q_im)
    ac_ref[0, 0, 0:4, :] = q_re
    ac_ref[0, 0, 4:8, :] = q_im


def _s5prep_call(ssm_a_re, ssm_a_im, ssm_log_dt, ssm_b_re, ssm_b_im, ssm_c_re, ssm_c_im):
    nb, gb = N_GROUP_BLOCKS, GROUPS_PER_BLOCK
    eye = jnp.eye(gb, dtype=F32)
    row = lambda a: a.reshape(DEPTH, nb, 1, STATE_LANES)
    quad = lambda a: a.reshape(DEPTH, nb, 4, LANES)
    ldt = jnp.broadcast_to(ssm_log_dt[:, :, None], (DEPTH, N_SSM_GROUPS, SSM_STATE))

    def bt_bd(b):
        bt = b.reshape(DEPTH, nb, gb, SSM_STATE, SSM_GROUP).transpose(0, 1, 2, 4, 3)
        return jnp.einsum('lngkp,gh->lngkhp', bt, eye).reshape(DEPTH, nb, LANES, STATE_LANES)

    def ct_bd(c):
        ct = c.reshape(DEPTH, nb, gb, SSM_GROUP, SSM_STATE).transpose(0, 1, 2, 4, 3)
        return jnp.einsum('lngpk,gh->lngphk', ct, eye).reshape(DEPTH, nb, STATE_LANES, LANES)

    def c_bd(c):
        cc = c.reshape(DEPTH, nb, gb, SSM_GROUP, SSM_STATE)
        return jnp.einsum('lngkp,gh->lngkhp', cc, eye).reshape(DEPTH, nb, LANES, STATE_LANES)

    blk = lambda *shape: pl.BlockSpec((1, 1) + shape, lambda l, g: (l, g) + (0,) * len(shape))
    return pl.pallas_call(
        _s5prep_kernel,
        grid=(DEPTH, nb),
        in_specs=[blk(1, STATE_LANES)] * 3 + [blk(4, LANES)] * 3
        + [blk(LANES, STATE_LANES)] * 2 + [blk(STATE_LANES, LANES)] * 2 + [blk(LANES, STATE_LANES)] * 2,
        out_specs=[blk(CHUNK, LANES, LANES), blk(CHUNK * LANES, 2 * STATE_LANES),
                   blk(CHUNK * LANES, 2 * STATE_LANES), blk(16, LANES)],
        out_shape=[
            jax.ShapeDtypeStruct((DEPTH, nb, CHUNK, LANES, LANES), BF16),
            jax.ShapeDtypeStruct((DEPTH, nb, CHUNK * LANES, 2 * STATE_LANES), BF16),
            jax.ShapeDtypeStruct((DEPTH, nb, CHUNK * LANES, 2 * STATE_LANES), BF16),
            jax.ShapeDtypeStruct((DEPTH, nb, 16, LANES), F32),
        ],
        compiler_params=_cp("parallel", "parallel"),
        name="s5_prep",
    )(row(ssm_a_re), row(ssm_a_im), row(ldt), quad(ssm_a_re), quad(ssm_a_im), quad(ldt),
      bt_bd(ssm_b_re), bt_bd(ssm_b_im), ct_bd(ssm_c_re), ct_bd(ssm_c_im), c_bd(ssm_c_re), c_bd(ssm_c_im))


def _gelu_tanh(x):
    return x * (0.5 * (1.0 + jnp.tanh(math.sqrt(2.0 / math.pi) * (x + 0.044715 * (x * x * x)))))


def _s5_kernel(us_ref, h0_ref, d_ref, bdz_ref, bdh_ref, ac_ref, dsk_ref,
               y_ref, ht_ref, bd_scr, ucat_scr, y_scr, zs_scr, hs_scr, hcat_scr, *, lc, nsq, nc):
    m = nsq * nc
    kd = lc * LANES
    first_b = pl.program_id(1) == 0

    @pl.when(jnp.logical_and(pl.program_id(0) == 0, first_b))
    def _():
        bd_scr[...] = jnp.zeros_like(bd_scr)

    @pl.when(first_b)
    def _():
        for tp in range(lc):
            for t in range(tp, lc):
                bd_scr[tp * LANES:(tp + 1) * LANES, t * LANES:(t + 1) * LANES] = d_ref[0, 0, t - tp]

    for t in range(lc):
        ucat_scr[:, t * LANES:(t + 1) * LANES] = us_ref[pl.ds(t, m, stride=lc), :].astype(BF16)
    ucat = ucat_scr[...]
    y_scr[...] = _dot(ucat, bd_scr[0:kd, 0:kd])
    z = _dot(ucat, bdz_ref[0, 0, (CHUNK - lc) * LANES:CHUNK * LANES, :])
    for s in range(8):
        zs_scr[:, s, :] = z[:, s * LANES:(s + 1) * LANES]

    r0 = 0 if lc == CHUNK else 8
    a_re = ac_ref[0, 0, r0:r0 + 4, :]
    a_im = ac_ref[0, 0, r0 + 4:r0 + 8, :]
    if nc == 1:
        h = h0_ref[:, 0]
        hs_scr[...] = h
        h_re, h_im = h[:, 0:4, :], h[:, 4:8, :]
        zz = zs_scr[...]
        ht_ref[:, 0, 0:4, :] = a_re * h_re - a_im * h_im + zz[:, 0:4, :]
        ht_ref[:, 0, 4:8, :] = a_re * h_im + a_im * h_re + zz[:, 4:8, :]
    else:
        def step(c, carry):
            h_re, h_im = carry
            hs_scr[c, 0:4, :] = h_re
            hs_scr[c, 4:8, :] = h_im
            zz = zs_scr[c]
            return (a_re * h_re - a_im * h_im + zz[0:4, :], a_re * h_im + a_im * h_re + zz[4:8, :])

        h_re, h_im = lax.fori_loop(0, nc, step, (h0_ref[0, 0, 0:4, :], h0_ref[0, 0, 4:8, :]))
        ht_ref[0, 0, 0:4, :] = h_re
        ht_ref[0, 0, 4:8, :] = h_im

    for s in range(8):
        hcat_scr[:, s * LANES:(s + 1) * LANES] = hs_scr[:, s, :].astype(BF16)
    y = y_scr[...] + lax.dot_general(hcat_scr[...], bdh_ref[0, 0, 0:kd, :], (((1,), (1,)), ((), ())),
                                     preferred_element_type=F32)
    for t in range(lc):
        u = us_ref[pl.ds(t, m, stride=lc), :]
        v = y[:, t * LANES:(t + 1) * LANES] + dsk_ref[...] * u
        y_ref[pl.ds(t, m, stride=lc), :] = _gelu_tanh(v).astype(y_ref.dtype)


def _s5_call(us, h0, dmat, bdz, bdh, acoef, dskip, layer, nseq, s):
    if s % CHUNK == 0:
        lc, nsq, nc = CHUNK, 1, s // CHUNK
    else:
        lc, nsq, nc = s, nseq, 1
    nb_seq = nseq // nsq
    m = nsq * nc
    rows = m * lc
    lay = lambda *shape: pl.BlockSpec((1, 1) + shape, lambda g, b: (layer, g) + (0,) * len(shape))
    kern = functools.partial(_s5_kernel, lc=lc, nsq=nsq, nc=nc)
    return pl.pallas_call(
        kern,
        grid=(N_GROUP_BLOCKS, nb_seq),
        in_specs=[
            pl.BlockSpec((rows, LANES), lambda g, b: (b, g)),
            pl.BlockSpec((nsq, 1, 8, LANES), lambda g, b: (b, g, 0, 0)),
            lay(CHUNK, LANES, LANES),
            lay(CHUNK * LANES, 2 * STATE_LANES),
            lay(CHUNK * LANES, 2 * STATE_LANES),
            lay(16, LANES),
            pl.BlockSpec((1, LANES), lambda g, b: (0, g)),
        ],
        out_specs=[
            pl.BlockSpec((rows, LANES), lambda g, b: (b, g)),
            pl.BlockSpec((nsq, 1, 8, LANES), lambda g, b: (b, g, 0, 0)),
        ],
        out_shape=[
            jax.ShapeDtypeStruct((nseq * s, D_SSM), F32),
            jax.ShapeDtypeStruct((nseq, N_GROUP_BLOCKS, 8, LANES), F32),
        ],
        scratch_shapes=[
            pltpu.VMEM((CHUNK * LANES, CHUNK * LANES), BF16),
            pltpu.VMEM((m, lc * LANES), BF16),
            pltpu.VMEM((m, lc * LANES), F32),
            pltpu.VMEM((m, 8, LANES), F32),
            pltpu.VMEM((m, 8, LANES), F32),
            pltpu.VMEM((m, 2 * STATE_LANES), BF16),
        ],
        compiler_params=_cp("arbitrary", "arbitrary"),
        name="s5_scan",
    )(us, h0, dmat, bdz, bdh, acoef, dskip)


def _state_to_blocks(h_re, h_im):
    nseq = h_re.shape[0]
    q = lambda a: a.reshape(nseq, N_GROUP_BLOCKS, 4, LANES)
    return jnp.concatenate([q(h_re), q(h_im)], axis=2)


def _blocks_to_state(hb):
    nseq = hb.shape[0]
    return (hb[:, :, 0:4, :].reshape(nseq, N_SSM_GROUPS, SSM_STATE),
            hb[:, :, 4:8, :].reshape(nseq, N_SSM_GROUPS, SSM_STATE))


HALO = CONV_WIDTH - 1
PAD0 = 32 - HALO


def _conv_kernel(u_ref, st_ref, w_ref, b_ref, g_ref, beta_ref, v_ref, nst_ref, xp_scr, cv_scr, *, rr):
    bs, ts, c = u_ref.shape
    first = pl.program_id(1) == 0
    for b in range(bs):
        @pl.when(first)
        def _():
            xp_scr[b, PAD0:32, :] = st_ref[b]

        xp_scr[b, 32:32 + ts, :] = u_ref[b]

        def body(r, carry):
            base = pl.multiple_of(r * rr, rr)
            for c0 in range(0, c, LANES):
                win = xp_scr[b, pl.ds(base, rr + 32), c0:c0 + LANES]
                acc = jnp.zeros((rr, LANES), F32)
                for ph in range(8):
                    offs = [PAD0 + k for k in range(CONV_WIDTH) if (PAD0 + k) % 8 == ph]
                    span = max(offs) - ph + rr
                    shifted = win[ph:ph + span]
                    for o in offs:
                        a = o - ph
                        acc = acc + w_ref[o - PAD0:o - PAD0 + 1, c0:c0 + LANES] * shifted[a:a + rr]
                cv_scr[:, c0:c0 + LANES] = acc
            y = _ln(cv_scr[...] + b_ref[...]) * g_ref[...] + beta_ref[...]
            v_ref[b, pl.ds(base, rr), :] = (y * _sigmoid(y)).astype(v_ref.dtype)
            return carry

        lax.fori_loop(0, ts // rr, body, 0)
        tail = xp_scr[b, ts + PAD0:ts + 32, :]
        nst_ref[b] = tail
        xp_scr[b, PAD0:32, :] = tail


def _conv_call(uc, state, conv_w, conv_b, ln_g, ln_b, bs, ts):
    nseq, s, c = uc.shape
    rr = min(ts, 64)
    vec = lambda a: a.reshape(1, c)
    cst = lambda shape: pl.BlockSpec(shape, lambda i, j: (0,) * len(shape))
    return pl.pallas_call(
        functools.partial(_conv_kernel, rr=rr),
        grid=(nseq // bs, s // ts),
        in_specs=[
            pl.BlockSpec((bs, ts, c), lambda i, j: (i, j, 0)),
            pl.BlockSpec((bs, HALO, c), lambda i, j: (i, 0, 0)),
            cst((CONV_WIDTH, c)), cst((1, c)), cst((1, c)), cst((1, c)),
        ],
        out_specs=[
            pl.BlockSpec((bs, ts, c), lambda i, j: (i, j, 0)),
            pl.BlockSpec((bs, HALO, c), lambda i, j: (i, 0, 0)),
        ],
        out_shape=[
            jax.ShapeDtypeStruct((nseq, s, c), F32),
            jax.ShapeDtypeStruct((nseq, HALO, c), F32),
        ],
        scratch_shapes=[pltpu.VMEM((bs, ts + 32, c), F32), pltpu.VMEM((rr, c), F32)],
        compiler_params=_cp("parallel", "arbitrary"),
        name="conv_module",
    )(uc, state, conv_w, vec(conv_b), vec(ln_g), vec(ln_b))


def _route(logits):
    lane = lax.broadcasted_iota(jnp.int32, logits.shape, 1).astype(F32)
    neg = jnp.float32(-jnp.inf)
    big = jnp.float32(1e9)
    gmask = jnp.logical_and(lane >= N_EXPERTS, lane < N_EXPERTS + N_EXPERT_GROUPS)
    gmax = jnp.max(jnp.where(gmask, logits, neg), axis=-1, keepdims=True)
    gsel = jnp.min(jnp.where(jnp.logical_and(gmask, logits == gmax), lane, big), axis=-1, keepdims=True)
    p_g = 1.0 / jnp.sum(jnp.where(gmask, jnp.exp(logits - gmax), 0.0), axis=-1, keepdims=True)
    lo = (gsel - N_EXPERTS) * EXPERTS_PER_GROUP
    emask = jnp.logical_and(lane >= lo, lane < lo + EXPERTS_PER_GROUP)
    v1 = jnp.max(jnp.where(emask, logits, neg), axis=-1, keepdims=True)
    i1 = jnp.min(jnp.where(jnp.logical_and(emask, logits == v1), lane, big), axis=-1, keepdims=True)
    emask2 = jnp.logical_and(emask, lane != i1)
    v2 = jnp.max(jnp.where(emask2, logits, neg), axis=-1, keepdims=True)
    i2 = jnp.min(jnp.where(jnp.logical_and(emask2, logits == v2), lane, big), axis=-1, keepdims=True)
    e2 = jnp.exp(v2 - v1)
    den = 1.0 + e2
    w1 = (1.0 / den) * p_g
    w2 = (e2 / den) * p_g
    comb = jnp.where(lane == i1, w1, 0.0) + jnp.where(lane == i2, w2, 0.0)
    return comb + jnp.where(lane == ROUTE_GROUP_LANE, gsel - N_EXPERTS, 0.0)


def _to_row_tiles(ref, x):
    for s in range(x.shape[-1] // LANES):
        ref[:, s, :] = x[:, s * LANES:(s + 1) * LANES]


def _from_row_tiles(ref, scr):
    for s in range(ref.shape[1]):
        scr[:, s * LANES:(s + 1) * LANES] = ref[:, s, :].astype(scr.dtype)


def _mix_kernel(ys_ref, v_ref, gs_ref, gc_ref, x_ref, g1_ref, sc2_ref, sh2_ref,
                wv_ref, wg_ref, wpw_ref, wo_ref, l1g_ref, l1b_ref, wr_ref, br_ref,
                x1_ref, h2_ref, comb_ref):
    bs, ts, d = x_ref.shape
    ys = ys_ref[...].astype(BF16)
    br_s = _dot(ys, wv_ref[...]) * _sigmoid(_dot(ys, wg_ref[...]))
    br_c = _dot(v_ref[...].astype(BF16), wpw_ref[...])
    mix = gs_ref[...].astype(F32) * br_s + gc_ref[...].astype(F32) * br_c
    o = _dot(mix.astype(BF16), wo_ref[...]).reshape(bs, ts, d)
    x1 = _ln(ALPHA * x_ref[...] + g1_ref[...] * o) * l1g_ref[...] + l1b_ref[...]
    x1_ref[...] = x1
    h2 = (_ln(x1) * (1.0 + sc2_ref[...]) + sh2_ref[...]).reshape(bs * ts, d)
    _to_row_tiles(h2_ref, h2)
    logits = jnp.dot(h2, wr_ref[...], preferred_element_type=F32,
                     precision=lax.Precision.HIGHEST) + br_ref[...]
    comb_ref[...] = _route(logits)


def _mix_call(ys, v, g, x, g1, sc2, sh2, w_val, w_gate, w_pw, w_out, ln_g, ln_b, w_r, b_r, bs, ts):
    nseq, s, d = x.shape
    ni, nj = nseq // bs, s // ts
    m = bs * ts
    t = nseq * s
    row0 = lambda i, j: (i * nj + j, 0)
    mod = pl.BlockSpec((bs, 1, d), lambda i, j: (i, 0, 0))
    res = lambda shape: pl.BlockSpec(shape, lambda i, j: (0,) * len(shape), pipeline_mode=pl.Buffered(1))
    return pl.pallas_call(
        _mix_kernel,
        grid=(ni, nj),
        in_specs=[
            pl.BlockSpec((m, D_SSM), row0),
            pl.BlockSpec((m, D_CONV), row0),
            pl.BlockSpec((m, d), row0),
            pl.BlockSpec((m, d), lambda i, j: (i * nj + j, 1)),
            pl.BlockSpec((bs, ts, d), lambda i, j: (i, j, 0)),
            mod, mod, mod,
            res((D_SSM, d)), res((D_SSM, d)), res((D_CONV, d)), res((d, d)),
            res((1, d)), res((1, d)), res((d, LANES)), res((1, LANES)),
        ],
        out_specs=[
            pl.BlockSpec((bs, ts, d), lambda i, j: (i, j, 0)),
            pl.BlockSpec((m, d // LANES, LANES), lambda i, j: (i * nj + j, 0, 0)),
            pl.BlockSpec((m, LANES), row0),
        ],
        out_shape=[
            jax.ShapeDtypeStruct((nseq, s, d), F32),
            jax.ShapeDtypeStruct((t, d // LANES, LANES), F32),
            jax.ShapeDtypeStruct((t, LANES), F32),
        ],
        compiler_params=_cp("parallel", "parallel"),
        name="branch_mix",
    )(ys, v, g, g, x, g1, sc2, sh2, w_val, w_gate, w_pw, w_out,
      ln_g.reshape(1, d), ln_b.reshape(1, d), w_r, b_r)


def _moe_plan(gid, tm):
    t = gid.shape[0]
    npad = t + N_EXPERT_GROUPS * tm
    onehot = (gid[:, None] == jnp.arange(N_EXPERT_GROUPS, dtype=jnp.int32)[None, :]).astype(jnp.int32)
    csum = jnp.cumsum(onehot, axis=0)
    rank = jnp.sum(csum * onehot, axis=1) - 1
    cnt = csum[-1]
    seg = ((cnt + tm - 1) // tm) * tm
    end = jnp.cumsum(seg)
    pos = (end - seg)[gid] + rank
    src = jnp.zeros((npad,), jnp.int32).at[pos].set(jnp.arange(t, dtype=jnp.int32), unique_indices=True)
    tile_start = jnp.arange(npad // tm, dtype=jnp.int32) * tm
    tile_group = jnp.minimum(jnp.sum((tile_start[:, None] >= end[None, :]).astype(jnp.int32), axis=1),
                             N_EXPERT_GROUPS - 1)
    n_tiles = (end[-1] // tm).reshape(1)
    return pos.astype(jnp.int32), src, tile_group.astype(jnp.int32), n_tiles.astype(jnp.int32)


def _gather_kernel(idx_ref, tab_ref, out_ref):
    rows = out_ref.shape[0]
    base = pl.program_id(1) * rows

    def body(r, carry):
        out_ref[r] = tab_ref[idx_ref[base + r]]
        return carry

    lax.fori_loop(0, rows, body, 0, unroll=8)


def _gather_call(table, idx, rows):
    n, nt, lanes = table.shape
    r_total = idx.shape[0]
    half = 8
    return pl.pallas_call(
        _gather_kernel,
        grid_spec=pltpu.PrefetchScalarGridSpec(
            num_scalar_prefetch=1,
            grid=(nt // half, r_total // rows),
            in_specs=[pl.BlockSpec((n, half, lanes), lambda h, i, idx_ref: (0, h, 0),
                                   pipeline_mode=pl.Buffered(1))],
            out_specs=pl.BlockSpec((rows, half, lanes), lambda h, i, idx_ref: (i, h, 0)),
        ),
        out_shape=jax.ShapeDtypeStruct((r_total, nt, lanes), table.dtype),
        compiler_params=_cp("arbitrary", "arbitrary"),
        name="row_gather",
    )(idx, table)


def _experts_kernel(tg_ref, nt_ref, xs_ref, cs_ref, wu_ref, wg_ref, wd_ref, ys_ref, h_scr, acc_scr):
    i = pl.program_id(0)
    j = pl.program_id(1)
    last = j == pl.num_programs(1) - 1
    live = i < nt_ref[0]

    @pl.when(live)
    def _():
        @pl.when(j == 0)
        def _():
            _from_row_tiles(xs_ref, h_scr)

        h = h_scr[...]
        up = _dot(h, wu_ref[0])
        gt = _dot(h, wg_ref[0])
        cs = cs_ref[...]
        lane = lax.broadcasted_iota(jnp.int32, cs.shape, 1)
        e = tg_ref[i] * EXPERTS_PER_GROUP + j
        ce = jnp.sum(jnp.where(lane == e, cs, 0.0), axis=-1, keepdims=True)
        act = (gt * _sigmoid(gt)) * up * ce
        contrib = _dot(act.astype(BF16), wd_ref[0])

        @pl.when(j == 0)
        def _():
            acc_scr[...] = contrib

        @pl.when(j > 0)
        def _():
            acc_scr[...] += contrib

        @pl.when(last)
        def _():
            _to_row_tiles(ys_ref, acc_scr[...])

    @pl.when(jnp.logical_and(jnp.logical_not(live), last))
    def _():
        ys_ref[...] = jnp.zeros_like(ys_ref)


def _experts_call(xs, cs, tile_group, n_tiles, w_up, w_gate, w_down, tm):
    npad, nt, lanes = xs.shape
    d = nt * lanes
    wmap = lambda i, j, tg, n: (tg[i] * EXPERTS_PER_GROUP + j, 0, 0)
    return pl.pallas_call(
        _experts_kernel,
        grid_spec=pltpu.PrefetchScalarGridSpec(
            num_scalar_prefetch=2,
            grid=(npad // tm, EXPERTS_PER_GROUP),
            in_specs=[
                pl.BlockSpec((tm, nt, lanes), lambda i, j, tg, n: (i, 0, 0)),
                pl.BlockSpec((tm, LANES), lambda i, j, tg, n: (i, 0)),
                pl.BlockSpec((1, d, D_EXPERT), wmap),
                pl.BlockSpec((1, d, D_EXPERT), wmap),
                pl.BlockSpec((1, D_EXPERT, d), wmap),
            ],
            out_specs=pl.BlockSpec((tm, nt, lanes), lambda i, j, tg, n: (i, 0, 0)),
            scratch_shapes=[pltpu.VMEM((tm, d), BF16), pltpu.VMEM((tm, d), F32)],
        ),
        out_shape=jax.ShapeDtypeStruct((npad, nt, lanes), F32),
        compiler_params=_cp("parallel", "arbitrary"),
        name="moe_experts",
    )(tile_group, n_tiles, xs, cs, w_up, w_gate, w_down)


def _final_kernel(ff_ref, x1_ref, g2_ref, l2g_ref, l2b_ref, x2_ref, ff_scr):
    bs, ts, d = x1_ref.shape
    _from_row_tiles(ff_ref, ff_scr)
    ff = ff_scr[...].reshape(bs, ts, d)
    x2_ref[...] = _ln(ALPHA * x1_ref[...] + g2_ref[...] * ff) * l2g_ref[...] + l2b_ref[...]


def _final_call(ff, x1, g2, ln_g, ln_b, tok0, bs, ts):
    nseq, s, d = x1.shape
    ni, nj = nseq // bs, s // ts
    m = bs * ts
    t0 = tok0 // m
    vec = pl.BlockSpec((1, d), lambda i, j: (0, 0))
    return pl.pallas_call(
        _final_kernel,
        grid=(ni, nj),
        in_specs=[
            pl.BlockSpec((m, d // LANES, LANES), lambda i, j: (t0 + i * nj + j, 0, 0)),
            pl.BlockSpec((bs, ts, d), lambda i, j: (i, j, 0)),
            pl.BlockSpec((bs, 1, d), lambda i, j: (i, 0, 0)),
            vec, vec,
        ],
        out_specs=pl.BlockSpec((bs, ts, d), lambda i, j: (i, j, 0)),
        out_shape=jax.ShapeDtypeStruct((nseq, s, d), F32),
        scratch_shapes=[pltpu.VMEM((m, d), F32)],
        compiler_params=_cp("parallel", "parallel"),
        name="moe_residual_ln",
    )(ff, x1, g2, ln_g.reshape(1, d), ln_b.reshape(1, d))


def kernel(x_prompt, x_sample, c_prompt, c_sample, state_ssm_re, state_ssm_im, state_conv, w_ada, b_ada, w_in, ssm_a_re, ssm_a_im, ssm_log_dt, ssm_b_re, ssm_b_im, ssm_c_re, ssm_c_im, ssm_d, w_s5_val, w_s5_gate, conv_w, conv_b, conv_ln_g, conv_ln_b, w_conv_pw, w_out, ln1_g, ln1_b, moe_w_group, moe_b_group, moe_w_router, moe_b_router, moe_w_up, moe_w_gate, moe_w_down, ln2_g, ln2_b):
    n_p, s_p = x_prompt.shape[0], x_prompt.shape[1]
    n_s, s_s = x_sample.shape[0], x_sample.shape[1]

    pad_rows = (-(n_p + n_s)) % 8
    c_all = jnp.concatenate([c_prompt, c_sample, jnp.zeros((pad_rows, D_MODEL), F32)], axis=0)
    ada = _ada_call(c_all, w_ada, b_ada)
    dmat, bdz, bdh, acoef = _s5prep_call(ssm_a_re, ssm_a_im, ssm_log_dt, ssm_b_re, ssm_b_im,
                                         ssm_c_re, ssm_c_im)

    w_us = w_in[:, :, :D_SSM].astype(BF16)
    w_ca = w_in[:, :, D_SSM:D_SSM + D_CONV].astype(BF16)
    w_cb = w_in[:, :, D_SSM + D_CONV:D_SSM + 2 * D_CONV].astype(BF16)
    w_g = w_in[:, :, D_SSM + 2 * D_CONV:].astype(BF16)
    w_val = w_s5_val.astype(BF16)
    w_gate = w_s5_gate.astype(BF16)
    w_pw = w_conv_pw.astype(BF16)
    w_o = w_out.astype(BF16)
    w_up = moe_w_up.astype(BF16)
    w_gt = moe_w_gate.astype(BF16)
    w_dn = moe_w_down.astype(BF16)
    zpad = LANES - N_EXPERTS - N_EXPERT_GROUPS
    w_r = jnp.concatenate([moe_w_router, moe_w_group, jnp.zeros((DEPTH, D_MODEL, zpad), F32)], axis=-1)
    b_r = jnp.concatenate([moe_b_router, moe_b_group, jnp.zeros((DEPTH, zpad), F32)], axis=-1)
    b_r = b_r.reshape(DEPTH, 1, LANES)
    dskip = ssm_d.reshape(DEPTH, 1, D_SSM)

    streams = [
        dict(x=x_prompt, rows=slice(0, n_p), n=n_p, s=s_p, bs=1, ts=512, mbs=1, mts=256,
             h0=jnp.zeros((n_p, N_GROUP_BLOCKS, 8, LANES), F32),
             conv=jnp.zeros((n_p, HALO, D_CONV), F32), cbs=1, cts=512, per_layer_state=False),
        dict(x=x_sample, rows=slice(n_p, n_p + n_s), n=n_s, s=s_s, bs=64, ts=s_s, mbs=32, mts=s_s,
             cbs=8, cts=s_s, per_layer_state=True),
    ]
    outs = [dict(re=[], im=[], conv=[]) for _ in streams]

    for l in range(DEPTH):
        mids = []
        for st, out in zip(streams, outs):
            n, s, bs, ts = st['n'], st['s'], st['bs'], st['ts']
            x = st['x']
            mod = lambda k: ada[l, st['rows'], k * D_MODEL:(k + 1) * D_MODEL].reshape(n, 1, D_MODEL)
            sh1, sc1, g1, sh2, sc2, g2 = (mod(k) for k in range(ADA_CHUNKS))
            if st['per_layer_state']:
                h0 = _state_to_blocks(state_ssm_re[l], state_ssm_im[l])
                cst = state_conv[l]
            else:
                h0, cst = st['h0'], st['conv']

            us, uc, g = _inproj_call(x, sc1, sh1, w_us[l], w_ca[l], w_cb[l], w_g[l], bs, ts)
            ys, ht = _s5_call(us, h0, dmat, bdz, bdh, acoef, dskip[l], l, n, s)
            v, nbuf = _conv_call(uc.reshape(n, s, D_CONV), cst, conv_w[l], conv_b[l],
                                 conv_ln_g[l], conv_ln_b[l], st['cbs'], st['cts'])
            x1, h2, comb = _mix_call(ys, v.reshape(n * s, D_CONV), g, x, g1, sc2, sh2,
                                     w_val[l], w_gate[l], w_pw[l], w_o[l], ln1_g[l], ln1_b[l],
                                     w_r[l], b_r[l], st['mbs'], st['mts'])
            mids.append((x1, g2, h2, comb))
            h_re, h_im = _blocks_to_state(ht)
            out['re'].append(h_re)
            out['im'].append(h_im)
            out['conv'].append(nbuf)

        h2_all = jnp.concatenate([m_[2] for m_ in mids], axis=0)
        comb_all = jnp.concatenate([m_[3] for m_ in mids], axis=0)
        gid = comb_all[:, ROUTE_GROUP_LANE].astype(jnp.int32)
        pos, src, tile_group, n_tiles = _moe_plan(gid, MOE_TILE)
        xs = _gather_call(h2_all, src, GATHER_ROWS)
        ysort = _experts_call(xs, comb_all[src], tile_group, n_tiles, w_up[l], w_gt[l], w_dn[l], MOE_TILE)
        ff = _gather_call(ysort, pos, GATHER_ROWS)
        tok0 = 0
        for st, (x1, g2, _, _) in zip(streams, mids):
            st['x'] = _final_call(ff, x1, g2, ln2_g[l], ln2_b[l], tok0, st['bs'], st['ts'])
            tok0 += st['n'] * st['s']

    stack = lambda xs: jnp.stack(xs, axis=0)
    po, so = outs
    return (streams[0]['x'], streams[1]['x'],
            stack(po['re']), stack(po['im']), stack(po['conv']),
            stack(so['re']), stack(so['im']), stack(so['conv']))
```

```python
import functools
import math

import jax
import jax.numpy as jnp
from jax import lax
from jax.experimental import pallas as pl
from jax.experimental.pallas import tpu as pltpu

F32 = jnp.float32
BF16 = jnp.bfloat16

D_MODEL = 2048
DEPTH = 4
D_SSM = 1024
SSM_GROUP = 16
N_SSM_GROUPS = 64
SSM_STATE = 64
D_CONV = 1024
CONV_WIDTH = 31
N_EXPERT_GROUPS = 4
EXPERTS_PER_GROUP = 8
N_EXPERTS = 32
D_EXPERT = 256
ALPHA = (2 * DEPTH) ** 0.25
LN_EPS = 1e-5
ADA_CHUNKS = 6

LANES = 128
CHUNK = 16
GROUPS_PER_BLOCK = LANES // SSM_GROUP
N_GROUP_BLOCKS = N_SSM_GROUPS // GROUPS_PER_BLOCK
STATE_LANES = GROUPS_PER_BLOCK * SSM_STATE
VMEM_LIMIT = 56 * 1024 * 1024
ROUTE_GROUP_LANE = N_EXPERTS
MOE_TILE = 256
GATHER_ROWS = 512


def _cp(*sem):
    return pltpu.CompilerParams(dimension_semantics=sem, vmem_limit_bytes=VMEM_LIMIT)


def _ln(x):
    mu = jnp.mean(x, axis=-1, keepdims=True)
    xc = x - mu
    var = jnp.mean(xc * xc, axis=-1, keepdims=True)
    return xc * lax.rsqrt(var + LN_EPS)


def _sigmoid(x):
    return jax.nn.sigmoid(x)


def _dot(a, b):
    return jnp.dot(a, b, preferred_element_type=F32)


def _ada_kernel(c_ref, w_ref, b_ref, o_ref):
    c = c_ref[...]
    s = (c * _sigmoid(c)).astype(BF16)
    o_ref[0] = _dot(s, w_ref[0].astype(BF16)) + b_ref[0]


def _ada_call(c_all, w_ada, b_ada):
    rows = c_all.shape[0]
    n_out = ADA_CHUNKS * D_MODEL
    tn = 1024
    return pl.pallas_call(
        _ada_kernel,
        grid=(DEPTH, n_out // tn),
        in_specs=[
            pl.BlockSpec((rows, D_MODEL), lambda l, n: (0, 0)),
            pl.BlockSpec((1, D_MODEL, tn), lambda l, n: (l, 0, n)),
            pl.BlockSpec((1, 1, tn), lambda l, n: (l, 0, n)),
        ],
        out_specs=pl.BlockSpec((1, rows, tn), lambda l, n: (l, 0, n)),
        out_shape=jax.ShapeDtypeStruct((DEPTH, rows, n_out), F32),
        compiler_params=_cp("parallel", "parallel"),
        name="ada_proj",
    )(c_all, w_ada, b_ada.reshape(DEPTH, 1, n_out))


def _inproj_kernel(x_ref, sc_ref, sh_ref, wus_ref, wca_ref, wcb_ref, wg_ref,
                   us_ref, uc_ref, g_ref, h_scr):
    bs, ts, d = x_ref.shape

    @pl.when(pl.program_id(2) == 0)
    def _():
        h = _ln(x_ref[...]) * (1.0 + sc_ref[...]) + sh_ref[...]
        h_scr[...] = h.reshape(bs * ts, d).astype(BF16)

    h = h_scr[...]
    us_ref[...] = _dot(h, wus_ref[...])
    uc_ref[...] = _dot(h, wca_ref[...]) * _sigmoid(_dot(h, wcb_ref[...]))
    g_ref[...] = _sigmoid(_dot(h, wg_ref[...])).astype(g_ref.dtype)


def _inproj_call(x, sc, sh, w_in, bs, ts):
    nseq, s, d = x.shape
    ni, nj, nn = nseq // bs, s // ts, 4
    m = bs * ts
    t = nseq * s
    cn = D_SSM // nn
    gn = 2 * D_MODEL // nn
    g0 = (D_SSM + 2 * D_CONV) // gn
    row = lambda i, j, n: (i * nj + j, n)
    return pl.pallas_call(
        _inproj_kernel,
        grid=(ni, nj, nn),
        in_specs=[
            pl.BlockSpec((bs, ts, d), lambda i, j, n: (i, j, 0)),
            pl.BlockSpec((bs, 1, d), lambda i, j, n: (i, 0, 0)),
            pl.BlockSpec((bs, 1, d), lambda i, j, n: (i, 0, 0)),
            pl.BlockSpec((d, cn), lambda i, j, n: (0, n)),
            pl.BlockSpec((d, cn), lambda i, j, n: (0, nn + n)),
            pl.BlockSpec((d, cn), lambda i, j, n: (0, 2 * nn + n)),
            pl.BlockSpec((d, gn), lambda i, j, n: (0, g0 + n)),
        ],
        out_specs=[
            pl.BlockSpec((m, cn), row),
            pl.BlockSpec((m, cn), row),
            pl.BlockSpec((m, gn), row),
        ],
        out_shape=[
            jax.ShapeDtypeStruct((t, D_SSM), F32),
            jax.ShapeDtypeStruct((t, D_CONV), F32),
            jax.ShapeDtypeStruct((t, 2 * D_MODEL), BF16),
        ],
        scratch_shapes=[pltpu.VMEM((m, d), BF16)],
        compiler_params=_cp("parallel", "parallel", "arbitrary"),
        name="in_proj",
    )(x, sc, sh, w_in, w_in, w_in, w_in)


def _cmul(ar, ai, br, bi):
    return ar * br - ai * bi, ar * bi + ai * br


def _discretize(a_re, a_im, log_dt):
    dt = jnp.exp(log_dt)
    mag = jnp.exp(dt * a_re)
    ang = dt * a_im
    ab_re = mag * jnp.cos(ang)
    ab_im = mag * jnp.sin(ang)
    den = a_re * a_re + a_im * a_im
    k_re = ((ab_re - 1.0) * a_re + ab_im * a_im) / den
    k_im = (ab_im * a_re - (ab_re - 1.0) * a_im) / den
    return ab_re, ab_im, k_re, k_im


def _s5prep_kernel(ar_ref, ai_ref, ldt_ref, a4r_ref, a4i_ref, ldt4_ref,
                   btr_ref, bti_ref, ctr_ref, cti_ref, cr_ref, ci_ref,
                   d_ref, bdz_ref, bdh_ref, ac_ref):
    ab_re, ab_im, k_re, k_im = _discretize(ar_ref[0, 0], ai_ref[0, 0], ldt_ref[0, 0])
    btr, bti = btr_ref[0, 0], bti_ref[0, 0]
    ctr, cti = ctr_ref[0, 0], cti_ref[0, 0]
    cr, ci = cr_ref[0, 0], ci_ref[0, 0]
    neg_cti = -cti

    pw_re = jnp.ones_like(ab_re)
    pw_im = jnp.zeros_like(ab_re)
    for m in range(CHUNK):
        w_re, w_im = _cmul(pw_re, pw_im, k_re, k_im)
        l_re = btr * w_re - bti * w_im
        l_im = btr * w_im + bti * w_re
        dm = (jnp.dot(l_re, ctr, preferred_element_type=F32, precision=lax.Precision.HIGHEST)
              + jnp.dot(l_im, neg_cti, preferred_element_type=F32, precision=lax.Precision.HIGHEST))
        d_ref[0, 0, m] = dm.astype(BF16)
        tp = CHUNK - 1 - m
        bdz_ref[0, 0, tp * LANES:(tp + 1) * LANES, 0:STATE_LANES] = l_re.astype(BF16)
        bdz_ref[0, 0, tp * LANES:(tp + 1) * LANES, STATE_LANES:2 * STATE_LANES] = l_im.astype(BF16)
        pw_re, pw_im = _cmul(pw_re, pw_im, ab_re, ab_im)
        bdh_ref[0, 0, m * LANES:(m + 1) * LANES, 0:STATE_LANES] = (cr * pw_re - ci * pw_im).astype(BF16)
        bdh_ref[0, 0, m * LANES:(m + 1) * LANES, STATE_LANES:2 * STATE_LANES] = (
            -(cr * pw_im + ci * pw_re)).astype(BF16)

    q_re, q_im, _, _ = _discretize(a4r_ref[0, 0], a4i_ref[0, 0], ldt4_ref[0, 0])
    for _ in range(3):
        q_re, q_im = _cmul(q_re, q_im, q_re, q_im)
    ac_ref[0, 0, 8:12, :] = q_re
    ac_ref[0, 0, 12:16, :] = q_im
    q_re, q_im = _cmul(q_re, q_im, q_re, q_im)
    ac_ref[0, 0, 0:4, :] = q_re
    ac_ref[0, 0, 4:8, :] = q_im


def _s5prep_call(ssm_a_re, ssm_a_im, ssm_log_dt, ssm_b_re, ssm_b_im, ssm_c_re, ssm_c_im):
    nb, gb = N_GROUP_BLOCKS, GROUPS_PER_BLOCK
    eye = jnp.eye(gb, dtype=F32)
    row = lambda a: a.reshape(DEPTH, nb, 1, STATE_LANES)
    quad = lambda a: a.reshape(DEPTH, nb, 4, LANES)
    ldt = jnp.broadcast_to(ssm_log_dt[:, :, None], (DEPTH, N_SSM_GROUPS, SSM_STATE))

    def bt_bd(b):
        bt = b.reshape(DEPTH, nb, gb, SSM_STATE, SSM_GROUP).transpose(0, 1, 2, 4, 3)
        return jnp.einsum('lngkp,gh->lngkhp', bt, eye).reshape(DEPTH, nb, LANES, STATE_LANES)

    def ct_bd(c):
        ct = c.reshape(DEPTH, nb, gb, SSM_GROUP, SSM_STATE).transpose(0, 1, 2, 4, 3)
        return jnp.einsum('lngpk,gh->lngphk', ct, eye).reshape(DEPTH, nb, STATE_LANES, LANES)

    def c_bd(c):
        cc = c.reshape(DEPTH, nb, gb, SSM_GROUP, SSM_STATE)
        return jnp.einsum('lngkp,gh->lngkhp', cc, eye).reshape(DEPTH, nb, LANES, STATE_LANES)

    blk = lambda *shape: pl.BlockSpec((1, 1) + shape, lambda l, g: (l, g) + (0,) * len(shape))
    return pl.pallas_call(
        _s5prep_kernel,
        grid=(DEPTH, nb),
        in_specs=[blk(1, STATE_LANES)] * 3 + [blk(4, LANES)] * 3
        + [blk(LANES, STATE_LANES)] * 2 + [blk(STATE_LANES, LANES)] * 2 + [blk(LANES, STATE_LANES)] * 2,
        out_specs=[blk(CHUNK, LANES, LANES), blk(CHUNK * LANES, 2 * STATE_LANES),
                   blk(CHUNK * LANES, 2 * STATE_LANES), blk(16, LANES)],
        out_shape=[
            jax.ShapeDtypeStruct((DEPTH, nb, CHUNK, LANES, LANES), BF16),
            jax.ShapeDtypeStruct((DEPTH, nb, CHUNK * LANES, 2 * STATE_LANES), BF16),
            jax.ShapeDtypeStruct((DEPTH, nb, CHUNK * LANES, 2 * STATE_LANES), BF16),
            jax.ShapeDtypeStruct((DEPTH, nb, 16, LANES), F32),
        ],
        compiler_params=_cp("parallel", "parallel"),
        name="s5_prep",
    )(row(ssm_a_re), row(ssm_a_im), row(ldt), quad(ssm_a_re), quad(ssm_a_im), quad(ldt),
      bt_bd(ssm_b_re), bt_bd(ssm_b_im), ct_bd(ssm_c_re), ct_bd(ssm_c_im), c_bd(ssm_c_re), c_bd(ssm_c_im))


def _gelu_tanh(x):
    return x * (0.5 * (1.0 + jnp.tanh(math.sqrt(2.0 / math.pi) * (x + 0.044715 * (x * x * x)))))


def _s5_kernel(us_ref, h0_ref, d_ref, bdz_ref, bdh_ref, ac_ref, dsk_ref,
               y_ref, ht_ref, bd_scr, ucat_scr, zs_scr, hs_scr, hcat_scr, *, lc, nsq, nc):
    m = nsq * nc
    kd = lc * LANES
    first_b = pl.program_id(1) == 0

    @pl.when(jnp.logical_and(pl.program_id(0) == 0, first_b))
    def _():
        bd_scr[...] = jnp.zeros_like(bd_scr)

    @pl.when(first_b)
    def _():
        for tp in range(lc):
            for t in range(tp, lc):
                bd_scr[tp * LANES:(tp + 1) * LANES, t * LANES:(t + 1) * LANES] = d_ref[0, 0, t - tp]

    for t in range(lc):
        ucat_scr[:, t * LANES:(t + 1) * LANES] = us_ref[pl.ds(t, m, stride=lc), :].astype(BF16)
    z = _dot(ucat_scr[...], bdz_ref[0, 0, (CHUNK - lc) * LANES:CHUNK * LANES, :])
    for s in range(8):
        zs_scr[:, s, :] = z[:, s * LANES:(s + 1) * LANES]

    r0 = 0 if lc == CHUNK else 8
    a_re = ac_ref[0, 0, r0:r0 + 4, :]
    a_im = ac_ref[0, 0, r0 + 4:r0 + 8, :]
    if nc == 1:
        h = h0_ref[:, 0]
        hs_scr[...] = h
        h_re, h_im = h[:, 0:4, :], h[:, 4:8, :]
        zz = zs_scr[...]
        ht_ref[:, 0, 0:4, :] = a_re * h_re - a_im * h_im + zz[:, 0:4, :]
        ht_ref[:, 0, 4:8, :] = a_re * h_im + a_im * h_re + zz[:, 4:8, :]
    else:
        def step(c, carry):
            new = []
            for q in range(nsq):
                h_re, h_im = carry[2 * q], carry[2 * q + 1]
                r = q * nc + c
                hs_scr[r, 0:4, :] = h_re
                hs_scr[r, 4:8, :] = h_im
                zz = zs_scr[r]
                new.append(a_re * h_re - a_im * h_im + zz[0:4, :])
                new.append(a_re * h_im + a_im * h_re + zz[4:8, :])
            return tuple(new)

        init = tuple(h0_ref[q, 0, r4:r4 + 4, :] for q in range(nsq) for r4 in (0, 4))
        fin = lax.fori_loop(0, nc, step, init)
        for q in range(nsq):
            ht_ref[q, 0, 0:4, :] = fin[2 * q]
            ht_ref[q, 0, 4:8, :] = fin[2 * q + 1]

    for s in range(8):
        hcat_scr[:, s * LANES:(s + 1) * LANES] = hs_scr[:, s, :].astype(BF16)
    y = _dot(ucat_scr[...], bd_scr[0:kd, 0:kd]) + lax.dot_general(
        hcat_scr[...], bdh_ref[0, 0, 0:kd, :], (((1,), (1,)), ((), ())), preferred_element_type=F32)
    for t in range(lc):
        u = us_ref[pl.ds(t, m, stride=lc), :]
        v = y[:, t * LANES:(t + 1) * LANES] + dsk_ref[...] * u
        y_ref[pl.ds(t, m, stride=lc), :] = _gelu_tanh(v).astype(y_ref.dtype)


def _s5_call(us, h0, dmat, bdz, bdh, acoef, dskip, layer, nseq, s):
    if s % CHUNK == 0:
        lc, nsq, nc = CHUNK, nseq, s // CHUNK
    else:
        lc, nsq, nc = s, nseq, 1
    nb_seq = nseq // nsq
    m = nsq * nc
    rows = m * lc
    lay = lambda *shape: pl.BlockSpec((1, 1) + shape, lambda g, b: (layer, g) + (0,) * len(shape),
                                      pipeline_mode=pl.Buffered(1))
    kern = functools.partial(_s5_kernel, lc=lc, nsq=nsq, nc=nc)
    return pl.pallas_call(
        kern,
        grid=(N_GROUP_BLOCKS, nb_seq),
        in_specs=[
            pl.BlockSpec((rows, LANES), lambda g, b: (b, g)),
            pl.BlockSpec((nsq, 1, 8, LANES), lambda g, b: (b, g, 0, 0)),
            lay(CHUNK, LANES, LANES),
            lay(CHUNK * LANES, 2 * STATE_LANES),
            lay(CHUNK * LANES, 2 * STATE_LANES),
            lay(16, LANES),
            pl.BlockSpec((1, LANES), lambda g, b: (0, g)),
        ],
        out_specs=[
            pl.BlockSpec((rows, LANES), lambda g, b: (b, g)),
            pl.BlockSpec((nsq, 1, 8, LANES), lambda g, b: (b, g, 0, 0)),
        ],
        out_shape=[
            jax.ShapeDtypeStruct((nseq * s, D_SSM), F32),
            jax.ShapeDtypeStruct((nseq, N_GROUP_BLOCKS, 8, LANES), F32),
        ],
        scratch_shapes=[
            pltpu.VMEM((CHUNK * LANES, CHUNK * LANES), BF16),
            pltpu.VMEM((m, lc * LANES), BF16),
            pltpu.VMEM((m, 8, LANES), F32),
            pltpu.VMEM((m, 8, LANES), F32),
            pltpu.VMEM((m, 2 * STATE_LANES), BF16),
        ],
        compiler_params=_cp("arbitrary", "arbitrary"),
        name="s5_scan",
    )(us, h0, dmat, bdz, bdh, acoef, dskip)


def _state_to_blocks(h_re, h_im):
    nseq = h_re.shape[0]
    q = lambda a: a.reshape(nseq, N_GROUP_BLOCKS, 4, LANES)
    return jnp.concatenate([q(h_re), q(h_im)], axis=2)


def _blocks_to_state(hb):
    nseq = hb.shape[0]
    return (hb[:, :, 0:4, :].reshape(nseq, N_SSM_GROUPS, SSM_STATE),
            hb[:, :, 4:8, :].reshape(nseq, N_SSM_GROUPS, SSM_STATE))


HALO = CONV_WIDTH - 1
PAD0 = 32 - HALO


def _conv_kernel(u_ref, st_ref, w_ref, b_ref, g_ref, beta_ref, v_ref, nst_ref, xp_scr, cv_scr, *, rr):
    bs, ts, c = u_ref.shape
    first = pl.program_id(1) == 0
    for b in range(bs):
        @pl.when(first)
        def _():
            xp_scr[b, PAD0:32, :] = st_ref[b]

        xp_scr[b, 32:32 + ts, :] = u_ref[b]

        def body(r, carry):
            base = pl.multiple_of(r * rr, rr)
            for c0 in range(0, c, LANES):
                win = xp_scr[b, pl.ds(base, rr + 32), c0:c0 + LANES]
                acc = jnp.zeros((rr, LANES), F32)
                for ph in range(8):
                    offs = [PAD0 + k for k in range(CONV_WIDTH) if (PAD0 + k) % 8 == ph]
                    span = max(offs) - ph + rr
                    shifted = win[ph:ph + span]
                    for o in offs:
                        a = o - ph
                        acc = acc + w_ref[o - PAD0:o - PAD0 + 1, c0:c0 + LANES] * shifted[a:a + rr]
                cv_scr[:, c0:c0 + LANES] = acc
            y = _ln(cv_scr[...] + b_ref[...]) * g_ref[...] + beta_ref[...]
            v_ref[b, pl.ds(base, rr), :] = (y * _sigmoid(y)).astype(v_ref.dtype)
            return carry

        lax.fori_loop(0, ts // rr, body, 0)
        tail = xp_scr[b, ts + PAD0:ts + 32, :]
        nst_ref[b] = tail
        xp_scr[b, PAD0:32, :] = tail


def _conv_call(uc, state, conv_w, conv_b, ln_g, ln_b, bs, ts):
    nseq, s, c = uc.shape
    rr = min(ts, 64)
    vec = lambda a: a.reshape(1, c)
    cst = lambda shape: pl.BlockSpec(shape, lambda i, j: (0,) * len(shape))
    return pl.pallas_call(
        functools.partial(_conv_kernel, rr=rr),
        grid=(nseq // bs, s // ts),
        in_specs=[
            pl.BlockSpec((bs, ts, c), lambda i, j: (i, j, 0)),
            pl.BlockSpec((bs, HALO, c), lambda i, j: (i, 0, 0)),
            cst((CONV_WIDTH, c)), cst((1, c)), cst((1, c)), cst((1, c)),
        ],
        out_specs=[
            pl.BlockSpec((bs, ts, c), lambda i, j: (i, j, 0)),
            pl.BlockSpec((bs, HALO, c), lambda i, j: (i, 0, 0)),
        ],
        out_shape=[
            jax.ShapeDtypeStruct((nseq, s, c), F32),
            jax.ShapeDtypeStruct((nseq, HALO, c), F32),
        ],
        scratch_shapes=[pltpu.VMEM((bs, ts + 32, c), F32), pltpu.VMEM((rr, c), F32)],
        compiler_params=_cp("parallel", "arbitrary"),
        name="conv_module",
    )(uc, state, conv_w, vec(conv_b), vec(ln_g), vec(ln_b))


def _route(logits):
    lane = lax.broadcasted_iota(jnp.int32, logits.shape, 1).astype(F32)
    neg = jnp.float32(-jnp.inf)
    big = jnp.float32(1e9)
    gmask = jnp.logical_and(lane >= N_EXPERTS, lane < N_EXPERTS + N_EXPERT_GROUPS)
    gmax = jnp.max(jnp.where(gmask, logits, neg), axis=-1, keepdims=True)
    gsel = jnp.min(jnp.where(jnp.logical_and(gmask, logits == gmax), lane, big), axis=-1, keepdims=True)
    p_g = 1.0 / jnp.sum(jnp.where(gmask, jnp.exp(logits - gmax), 0.0), axis=-1, keepdims=True)
    lo = (gsel - N_EXPERTS) * EXPERTS_PER_GROUP
    emask = jnp.logical_and(lane >= lo, lane < lo + EXPERTS_PER_GROUP)
    v1 = jnp.max(jnp.where(emask, logits, neg), axis=-1, keepdims=True)
    i1 = jnp.min(jnp.where(jnp.logical_and(emask, logits == v1), lane, big), axis=-1, keepdims=True)
    emask2 = jnp.logical_and(emask, lane != i1)
    v2 = jnp.max(jnp.where(emask2, logits, neg), axis=-1, keepdims=True)
    i2 = jnp.min(jnp.where(jnp.logical_and(emask2, logits == v2), lane, big), axis=-1, keepdims=True)
    e2 = jnp.exp(v2 - v1)
    den = 1.0 + e2
    w1 = (1.0 / den) * p_g
    w2 = (e2 / den) * p_g
    comb = jnp.where(lane == i1, w1, 0.0) + jnp.where(lane == i2, w2, 0.0)
    return comb + jnp.where(lane == ROUTE_GROUP_LANE, gsel - N_EXPERTS, 0.0)


def _to_row_tiles(ref, x, rows=slice(None)):
    for s in range(x.shape[-1] // LANES):
        ref[rows, s, :] = x[:, s * LANES:(s + 1) * LANES]


def _from_row_tiles(ref, scr):
    for s in range(ref.shape[1]):
        scr[:, s * LANES:(s + 1) * LANES] = ref[:, s, :].astype(scr.dtype)


def _mix_kernel(ys_ref, v_ref, gs_ref, gc_ref, x_ref, g1_ref, sc2_ref, sh2_ref,
                wv_ref, wg_ref, wpw_ref, wo_ref, l1g_ref, l1b_ref, wrh_ref, wrl_ref, br_ref,
                x1_ref, h2_ref, comb_ref):
    bs, ts, d = x_ref.shape
    mh = bs * ts // 2
    for k in range(2):
        if bs > 1:
            bsl, tsl = slice(k * bs // 2, (k + 1) * bs // 2), slice(0, ts)
        else:
            bsl, tsl = slice(0, bs), slice(k * ts // 2, (k + 1) * ts // 2)
        rows = slice(k * mh, (k + 1) * mh)
        ys = ys_ref[rows, :].astype(BF16)
        br_s = _dot(ys, wv_ref[...]) * _sigmoid(_dot(ys, wg_ref[...]))
        br_c = _dot(v_ref[rows, :].astype(BF16), wpw_ref[...])
        mix = gs_ref[rows, :].astype(F32) * br_s + gc_ref[rows, :].astype(F32) * br_c
        x = x_ref[bsl, tsl, :]
        o = _dot(mix.astype(BF16), wo_ref[...]).reshape(x.shape)
        x1 = _ln(ALPHA * x + g1_ref[bsl] * o) * l1g_ref[...] + l1b_ref[...]
        x1_ref[bsl, tsl, :] = x1
        h2 = (_ln(x1) * (1.0 + sc2_ref[bsl]) + sh2_ref[bsl]).reshape(mh, d)
        _to_row_tiles(h2_ref, h2, rows)
        hi = h2.astype(BF16)
        lo = (h2 - hi.astype(F32)).astype(BF16)
        logits = (_dot(hi, wrh_ref[...]) + _dot(hi, wrl_ref[...]) + _dot(lo, wrh_ref[...])
                  + br_ref[...])
        comb_ref[rows, :] = _route(logits)


def _mix_call(ys, v, g, x, g1, sc2, sh2, w_val, w_gate, w_pw, w_out, ln_g, ln_b, w_r_hi, w_r_lo, b_r,
              bs, ts):
    nseq, s, d = x.shape
    ni, nj = nseq // bs, s // ts
    m = bs * ts
    t = nseq * s
    row0 = lambda i, j: (i * nj + j, 0)
    mod = pl.BlockSpec((bs, 1, d), lambda i, j: (i, 0, 0))
    res = lambda shape: pl.BlockSpec(shape, lambda i, j: (0,) * len(shape), pipeline_mode=pl.Buffered(1))
    return pl.pallas_call(
        _mix_kernel,
        grid=(ni, nj),
        in_specs=[
            pl.BlockSpec((m, D_SSM), row0),
            pl.BlockSpec((m, D_CONV), row0),
            pl.BlockSpec((m, d), row0),
            pl.BlockSpec((m, d), lambda i, j: (i * nj + j, 1)),
            pl.BlockSpec((bs, ts, d), lambda i, j: (i, j, 0)),
            mod, mod, mod,
            res((D_SSM, d)), res((D_SSM, d)), res((D_CONV, d)), res((d, d)),
            res((1, d)), res((1, d)), res((d, LANES)), res((d, LANES)), res((1, LANES)),
        ],
        out_specs=[
            pl.BlockSpec((bs, ts, d), lambda i, j: (i, j, 0)),
            pl.BlockSpec((m, d // LANES, LANES), lambda i, j: (i * nj + j, 0, 0)),
            pl.BlockSpec((m, LANES), row0),
        ],
        out_shape=[
            jax.ShapeDtypeStruct((nseq, s, d), F32),
            jax.ShapeDtypeStruct((t, d // LANES, LANES), F32),
            jax.ShapeDtypeStruct((t, LANES), F32),
        ],
        compiler_params=_cp("parallel", "parallel"),
        name="branch_mix",
    )(ys, v, g, g, x, g1, sc2, sh2, w_val, w_gate, w_pw, w_out,
      ln_g.reshape(1, d), ln_b.reshape(1, d), w_r_hi, w_r_lo, b_r)


def _moe_plan(gid, tm):
    t = gid.shape[0]
    npad = t + N_EXPERT_GROUPS * tm
    onehot = (gid[:, None] == jnp.arange(N_EXPERT_GROUPS, dtype=jnp.int32)[None, :]).astype(jnp.int32)
    csum = jnp.cumsum(onehot, axis=0)
    rank = jnp.sum(csum * onehot, axis=1) - 1
    cnt = csum[-1]
    seg = ((cnt + tm - 1) // tm) * tm
    end = jnp.cumsum(seg)
    pos = (end - seg)[gid] + rank
    src = jnp.zeros((npad,), jnp.int32).at[pos].set(jnp.arange(t, dtype=jnp.int32), unique_indices=True)
    tile_start = jnp.arange(npad // tm, dtype=jnp.int32) * tm
    tile_group = jnp.minimum(jnp.sum((tile_start[:, None] >= end[None, :]).astype(jnp.int32), axis=1),
                             N_EXPERT_GROUPS - 1)
    n_tiles = (end[-1] // tm).reshape(1)
    return pos.astype(jnp.int32), src, tile_group.astype(jnp.int32), n_tiles.astype(jnp.int32)


def _gather_kernel(idx_ref, *refs):
    tabs, out_ref = refs[:-1], refs[-1]
    rows = out_ref.shape[0]
    base = pl.program_id(1) * rows

    def body(r, carry):
        t = idx_ref[base + r]
        lo = 0
        val = None
        for tab in tabs:
            n = tab.shape[0]
            cand = tab[jnp.clip(t - lo, 0, n - 1)]
            val = cand if val is None else jnp.where(t >= lo, cand, val)
            lo += n
        out_ref[r] = val
        return carry

    lax.fori_loop(0, rows, body, 0, unroll=8)


def _gather_call(tables, idx, rows):
    _, nt, lanes = tables[0].shape
    r_total = idx.shape[0]
    half = 8
    return pl.pallas_call(
        _gather_kernel,
        grid_spec=pltpu.PrefetchScalarGridSpec(
            num_scalar_prefetch=1,
            grid=(nt // half, r_total // rows),
            in_specs=[pl.BlockSpec((tab.shape[0], half, lanes), lambda h, i, idx_ref: (0, h, 0),
                                   pipeline_mode=pl.Buffered(1)) for tab in tables],
            out_specs=pl.BlockSpec((rows, half, lanes), lambda h, i, idx_ref: (i, h, 0)),
        ),
        out_shape=jax.ShapeDtypeStruct((r_total, nt, lanes), tables[0].dtype),
        compiler_params=_cp("arbitrary", "arbitrary"),
        name="row_gather",
    )(idx, *tables)


def _experts_kernel(tg_ref, nt_ref, xs_ref, cs_ref, wu_ref, wg_ref, wd_ref, ys_ref, h_scr, act_scr):
    i = pl.program_id(0)
    live = i < nt_ref[0]

    @pl.when(live)
    def _():
        _from_row_tiles(xs_ref, h_scr)
        h = h_scr[...]
        cs = cs_ref[...]
        lane = lax.broadcasted_iota(jnp.int32, cs.shape, 1)
        e0 = tg_ref[i] * EXPERTS_PER_GROUP
        for j in range(EXPERTS_PER_GROUP):
            cols = slice(j * D_EXPERT, (j + 1) * D_EXPERT)
            up = _dot(h, wu_ref[:, cols])
            gt = _dot(h, wg_ref[:, cols])
            ce = jnp.sum(jnp.where(lane == e0 + j, cs, 0.0), axis=-1, keepdims=True)
            act_scr[:, cols] = ((gt * _sigmoid(gt)) * up * ce).astype(BF16)
        _to_row_tiles(ys_ref, _dot(act_scr[...], wd_ref[...]))

    @pl.when(jnp.logical_not(live))
    def _():
        ys_ref[...] = jnp.zeros_like(ys_ref)


def _experts_call(xs, cs, tile_group, n_tiles, w_up, w_gate, w_down, tm):
    npad, nt, lanes = xs.shape
    d = nt * lanes
    gf = EXPERTS_PER_GROUP * D_EXPERT
    one = pl.Buffered(1)
    return pl.pallas_call(
        _experts_kernel,
        grid_spec=pltpu.PrefetchScalarGridSpec(
            num_scalar_prefetch=2,
            grid=(npad // tm,),
            in_specs=[
                pl.BlockSpec((tm, nt, lanes), lambda i, tg, n: (i, 0, 0)),
                pl.BlockSpec((tm, LANES), lambda i, tg, n: (i, 0)),
                pl.BlockSpec((d, gf), lambda i, tg, n: (0, tg[i]), pipeline_mode=one),
                pl.BlockSpec((d, gf), lambda i, tg, n: (0, tg[i]), pipeline_mode=one),
                pl.BlockSpec((gf, d), lambda i, tg, n: (tg[i], 0), pipeline_mode=one),
            ],
            out_specs=pl.BlockSpec((tm, nt, lanes), lambda i, tg, n: (i, 0, 0)),
            scratch_shapes=[pltpu.VMEM((tm, d), BF16), pltpu.VMEM((tm, gf), BF16)],
        ),
        out_shape=jax.ShapeDtypeStruct((npad, nt, lanes), F32),
        compiler_params=_cp("arbitrary"),
        name="moe_experts",
    )(tile_group, n_tiles, xs, cs, w_up, w_gate, w_down)


def _final_kernel(ff_ref, x1_ref, g2_ref, l2g_ref, l2b_ref, x2_ref, ff_scr):
    bs, ts, d = x1_ref.shape
    _from_row_tiles(ff_ref, ff_scr)
    ff = ff_scr[...].reshape(bs, ts, d)
    x2_ref[...] = _ln(ALPHA * x1_ref[...] + g2_ref[...] * ff) * l2g_ref[...] + l2b_ref[...]


def _final_call(ff, x1, g2, ln_g, ln_b, tok0, bs, ts):
    nseq, s, d = x1.shape
    ni, nj = nseq // bs, s // ts
    m = bs * ts
    t0 = tok0 // m
    vec = pl.BlockSpec((1, d), lambda i, j: (0, 0))
    return pl.pallas_call(
        _final_kernel,
        grid=(ni, nj),
        in_specs=[
            pl.BlockSpec((m, d // LANES, LANES), lambda i, j: (t0 + i * nj + j, 0, 0)),
            pl.BlockSpec((bs, ts, d), lambda i, j: (i, j, 0)),
            pl.BlockSpec((bs, 1, d), lambda i, j: (i, 0, 0)),
            vec, vec,
        ],
        out_specs=pl.BlockSpec((bs, ts, d), lambda i, j: (i, j, 0)),
        out_shape=jax.ShapeDtypeStruct((nseq, s, d), F32),
        scratch_shapes=[pltpu.VMEM((m, d), F32)],
        compiler_params=_cp("parallel", "parallel"),
        name="moe_residual_ln",
    )(ff, x1, g2, ln_g.reshape(1, d), ln_b.reshape(1, d))


def kernel(x_prompt, x_sample, c_prompt, c_sample, state_ssm_re, state_ssm_im, state_conv, w_ada, b_ada, w_in, ssm_a_re, ssm_a_im, ssm_log_dt, ssm_b_re, ssm_b_im, ssm_c_re, ssm_c_im, ssm_d, w_s5_val, w_s5_gate, conv_w, conv_b, conv_ln_g, conv_ln_b, w_conv_pw, w_out, ln1_g, ln1_b, moe_w_group, moe_b_group, moe_w_router, moe_b_router, moe_w_up, moe_w_gate, moe_w_down, ln2_g, ln2_b):
    n_p, s_p = x_prompt.shape[0], x_prompt.shape[1]
    n_s, s_s = x_sample.shape[0], x_sample.shape[1]

    pad_rows = (-(n_p + n_s)) % 8
    c_all = jnp.concatenate([c_prompt, c_sample, jnp.zeros((pad_rows, D_MODEL), F32)], axis=0)
    ada = _ada_call(c_all, w_ada, b_ada)
    dmat, bdz, bdh, acoef = _s5prep_call(ssm_a_re, ssm_a_im, ssm_log_dt, ssm_b_re, ssm_b_im,
                                         ssm_c_re, ssm_c_im)

    w_inb = w_in.astype(BF16)
    w_val = w_s5_val.astype(BF16)
    w_gate = w_s5_gate.astype(BF16)
    w_pw = w_conv_pw.astype(BF16)
    w_o = w_out.astype(BF16)
    ef = N_EXPERTS * D_EXPERT
    w_up = moe_w_up.astype(BF16).transpose(0, 2, 1, 3).reshape(DEPTH, D_MODEL, ef)
    w_gt = moe_w_gate.astype(BF16).transpose(0, 2, 1, 3).reshape(DEPTH, D_MODEL, ef)
    w_dn = moe_w_down.astype(BF16).reshape(DEPTH, ef, D_MODEL)
    zpad = LANES - N_EXPERTS - N_EXPERT_GROUPS
    w_r = jnp.concatenate([moe_w_router, moe_w_group, jnp.zeros((DEPTH, D_MODEL, zpad), F32)], axis=-1)
    w_r_hi = w_r.astype(BF16)
    w_r_lo = (w_r - w_r_hi.astype(F32)).astype(BF16)
    b_r = jnp.concatenate([moe_b_router, moe_b_group, jnp.zeros((DEPTH, zpad), F32)], axis=-1)
    b_r = b_r.reshape(DEPTH, 1, LANES)
    dskip = ssm_d.reshape(DEPTH, 1, D_SSM)

    streams = [
        dict(x=x_prompt, rows=slice(0, n_p), n=n_p, s=s_p, bs=1, ts=512, mbs=1, mts=256,
             h0=jnp.zeros((n_p, N_GROUP_BLOCKS, 8, LANES), F32),
             conv=jnp.zeros((n_p, HALO, D_CONV), F32), cbs=1, cts=512, per_layer_state=False),
        dict(x=x_sample, rows=slice(n_p, n_p + n_s), n=n_s, s=s_s, bs=64, ts=s_s, mbs=32, mts=s_s,
             cbs=8, cts=s_s, per_layer_state=True),
    ]
    outs = [dict(re=[], im=[], conv=[]) for _ in streams]

    for l in range(DEPTH):
        mids = []
        for st, out in zip(streams, outs):
            n, s, bs, ts = st['n'], st['s'], st['bs'], st['ts']
            x = st['x']
            mod = lambda k: ada[l, st['rows'], k * D_MODEL:(k + 1) * D_MODEL].reshape(n, 1, D_MODEL)
            sh1, sc1, g1, sh2, sc2, g2 = (mod(k) for k in range(ADA_CHUNKS))
            if st['per_layer_state']:
                h0 = _state_to_blocks(state_ssm_re[l], state_ssm_im[l])
                cst = state_conv[l]
            else:
                h0, cst = st['h0'], st['conv']

            us, uc, g = _inproj_call(x, sc1, sh1, w_inb[l], bs, ts)
            ys, ht = _s5_call(us, h0, dmat, bdz, bdh, acoef, dskip[l], l, n, s)
            v, nbuf = _conv_call(uc.reshape(n, s, D_CONV), cst, conv_w[l], conv_b[l],
                                 conv_ln_g[l], conv_ln_b[l], st['cbs'], st['cts'])
            x1, h2, comb = _mix_call(ys, v.reshape(n * s, D_CONV), g, x, g1, sc2, sh2,
                                     w_val[l], w_gate[l], w_pw[l], w_o[l], ln1_g[l], ln1_b[l],
                                     w_r_hi[l], w_r_lo[l], b_r[l], st['mbs'], st['mts'])
            mids.append((x1, g2, h2, comb))
            h_re, h_im = _blocks_to_state(ht)
            out['re'].append(h_re)
            out['im'].append(h_im)
            out['conv'].append(nbuf)

        comb_all = jnp.concatenate([m_[3] for m_ in mids], axis=0)
        gid = comb_all[:, ROUTE_GROUP_LANE].astype(jnp.int32)
        pos, src, tile_group, n_tiles = _moe_plan(gid, MOE_TILE)
        xs = _gather_call([m_[2] for m_ in mids], src, GATHER_ROWS)
        ysort = _experts_call(xs, comb_all[src], tile_group, n_tiles, w_up[l], w_gt[l], w_dn[l], MOE_TILE)
        ff = _gather_call([ysort], pos, GATHER_ROWS)
        tok0 = 0
        for st, (x1, g2, _, _) in zip(streams, mids):
            st['x'] = _final_call(ff, x1, g2, ln2_g[l], ln2_b[l], tok0, st['bs'], st['ts'])
            tok0 += st['n'] * st['s']

    stack = lambda xs: jnp.stack(xs, axis=0)
    po, so = outs
    return (streams[0]['x'], streams[1]['x'],
            stack(po['re']), stack(po['im']), stack(po['conv']),
            stack(so['re']), stack(so['im']), stack(so['conv']))
```

```python
import functools
import math

import jax
import jax.numpy as jnp
from jax import lax
from jax.experimental import pallas as pl
from jax.experimental.pallas import tpu as pltpu

F32 = jnp.float32
BF16 = jnp.bfloat16

D_MODEL = 2048
DEPTH = 4
D_SSM = 1024
SSM_GROUP = 16
N_SSM_GROUPS = 64
SSM_STATE = 64
D_CONV = 1024
CONV_WIDTH = 31
N_EXPERT_GROUPS = 4
EXPERTS_PER_GROUP = 8
N_EXPERTS = 32
D_EXPERT = 256
ALPHA = (2 * DEPTH) ** 0.25
LN_EPS = 1e-5
ADA_CHUNKS = 6

LANES = 128
CHUNK = 16
GROUPS_PER_BLOCK = LANES // SSM_GROUP
N_GROUP_BLOCKS = N_SSM_GROUPS // GROUPS_PER_BLOCK
STATE_LANES = GROUPS_PER_BLOCK * SSM_STATE
VMEM_LIMIT = 56 * 1024 * 1024
ROUTE_GROUP_LANE = N_EXPERTS
MOE_TILE = 256
GATHER_ROWS = 512


def _cp(*sem):
    return pltpu.CompilerParams(dimension_semantics=sem, vmem_limit_bytes=VMEM_LIMIT)


def _ln(x):
    mu = jnp.mean(x, axis=-1, keepdims=True)
    xc = x - mu
    var = jnp.mean(xc * xc, axis=-1, keepdims=True)
    return xc * lax.rsqrt(var + LN_EPS)


def _sigmoid(x):
    return jax.nn.sigmoid(x)


def _dot(a, b):
    return jnp.dot(a, b, preferred_element_type=F32)


def _split_bf16(x):
    hi = x.astype(BF16)
    return hi, (x - hi.astype(F32)).astype(BF16)


def _dot3(a, b_hi, b_lo):
    a_hi, a_lo = _split_bf16(a)
    return _dot(a_hi, b_hi) + _dot(a_hi, b_lo) + _dot(a_lo, b_hi)


CAST_BLOCK_BYTES = 8 * 1024 * 1024


def _cast_kernel(x_ref, o_ref):
    o_ref[...] = x_ref[...].astype(o_ref.dtype)


def _cast_call(x, dtype):
    nl, r, c = x.shape
    rb = min(r, 2 ** int(math.log2(CAST_BLOCK_BYTES // (c * x.dtype.itemsize))))
    blk = pl.BlockSpec((1, rb, c), lambda l, i: (l, i, 0))
    return pl.pallas_call(
        _cast_kernel, grid=(nl, r // rb), in_specs=[blk], out_specs=blk,
        out_shape=jax.ShapeDtypeStruct(x.shape, dtype),
        compiler_params=_cp("parallel", "parallel"), name="cast",
    )(x)


def _cast_experts_kernel(x_ref, o_ref):
    for j in range(x_ref.shape[1]):
        f = x_ref.shape[3]
        o_ref[0, :, j * f:(j + 1) * f] = x_ref[0, j].astype(o_ref.dtype)


def _cast_experts_call(w, dtype):
    nl, ne, d, f = w.shape
    eb = min(ne, CAST_BLOCK_BYTES // (d * f * w.dtype.itemsize))
    return pl.pallas_call(
        _cast_experts_kernel,
        grid=(nl, ne // eb),
        in_specs=[pl.BlockSpec((1, eb, d, f), lambda l, e: (l, e, 0, 0))],
        out_specs=pl.BlockSpec((1, d, eb * f), lambda l, e: (l, 0, e)),
        out_shape=jax.ShapeDtypeStruct((nl, d, ne * f), dtype),
        compiler_params=_cp("parallel", "parallel"), name="cast_experts",
    )(w)


def _ada_kernel(c_ref, w_ref, b_ref, o_ref):
    c = c_ref[...]
    s = (c * _sigmoid(c)).astype(BF16)
    o_ref[0] = _dot(s, w_ref[0].astype(BF16)) + b_ref[0]


def _ada_call(c_all, w_ada, b_ada):
    rows = c_all.shape[0]
    n_out = ADA_CHUNKS * D_MODEL
    tn = 1024
    return pl.pallas_call(
        _ada_kernel,
        grid=(DEPTH, n_out // tn),
        in_specs=[
            pl.BlockSpec((rows, D_MODEL), lambda l, n: (0, 0)),
            pl.BlockSpec((1, D_MODEL, tn), lambda l, n: (l, 0, n)),
            pl.BlockSpec((1, 1, tn), lambda l, n: (l, 0, n)),
        ],
        out_specs=pl.BlockSpec((1, rows, tn), lambda l, n: (l, 0, n)),
        out_shape=jax.ShapeDtypeStruct((DEPTH, rows, n_out), F32),
        compiler_params=_cp("parallel", "parallel"),
        name="ada_proj",
    )(c_all, w_ada, b_ada.reshape(DEPTH, 1, n_out))


def _inproj_kernel(x_ref, sc_ref, sh_ref, wus_ref, wca_ref, wcb_ref, wg_ref,
                   us_ref, uc_ref, g_ref, h_scr):
    bs, ts, d = x_ref.shape

    @pl.when(pl.program_id(2) == 0)
    def _():
        h = _ln(x_ref[...]) * (1.0 + sc_ref[...]) + sh_ref[...]
        h_scr[...] = h.reshape(bs * ts, d).astype(BF16)

    h = h_scr[...]
    us_ref[...] = _dot(h, wus_ref[...])
    uc_ref[...] = _dot(h, wca_ref[...]) * _sigmoid(_dot(h, wcb_ref[...]))
    g_ref[...] = _sigmoid(_dot(h, wg_ref[...])).astype(g_ref.dtype)


def _inproj_call(x, sc, sh, w_in, bs, ts):
    nseq, s, d = x.shape
    ni, nj, nn = nseq // bs, s // ts, 4
    m = bs * ts
    t = nseq * s
    cn = D_SSM // nn
    gn = 2 * D_MODEL // nn
    g0 = (D_SSM + 2 * D_CONV) // gn
    row = lambda i, j, n: (i * nj + j, n)
    return pl.pallas_call(
        _inproj_kernel,
        grid=(ni, nj, nn),
        in_specs=[
            pl.BlockSpec((bs, ts, d), lambda i, j, n: (i, j, 0)),
            pl.BlockSpec((bs, 1, d), lambda i, j, n: (i, 0, 0)),
            pl.BlockSpec((bs, 1, d), lambda i, j, n: (i, 0, 0)),
            pl.BlockSpec((d, cn), lambda i, j, n: (0, n)),
            pl.BlockSpec((d, cn), lambda i, j, n: (0, nn + n)),
            pl.BlockSpec((d, cn), lambda i, j, n: (0, 2 * nn + n)),
            pl.BlockSpec((d, gn), lambda i, j, n: (0, g0 + n)),
        ],
        out_specs=[
            pl.BlockSpec((m, cn), row),
            pl.BlockSpec((m, cn), row),
            pl.BlockSpec((m, gn), row),
        ],
        out_shape=[
            jax.ShapeDtypeStruct((t, D_SSM), F32),
            jax.ShapeDtypeStruct((t, D_CONV), F32),
            jax.ShapeDtypeStruct((t, 2 * D_MODEL), BF16),
        ],
        scratch_shapes=[pltpu.VMEM((m, d), BF16)],
        compiler_params=_cp("parallel", "parallel", "arbitrary"),
        name="in_proj",
    )(x, sc, sh, w_in, w_in, w_in, w_in)


def _cmul(ar, ai, br, bi):
    return ar * br - ai * bi, ar * bi + ai * br


def _discretize(a_re, a_im, log_dt):
    dt = jnp.exp(log_dt)
    mag = jnp.exp(dt * a_re)
    ang = dt * a_im
    ab_re = mag * jnp.cos(ang)
    ab_im = mag * jnp.sin(ang)
    den = a_re * a_re + a_im * a_im
    k_re = ((ab_re - 1.0) * a_re + ab_im * a_im) / den
    k_im = (ab_im * a_re - (ab_re - 1.0) * a_im) / den
    return ab_re, ab_im, k_re, k_im


def _s5prep_kernel(ar_ref, ai_ref, ldt_ref, a4r_ref, a4i_ref, ldt4_ref,
                   btr_ref, bti_ref, ctr_ref, cti_ref, cr_ref, ci_ref,
                   d_ref, bdz_ref, bdh_ref, ac_ref):
    ab_re, ab_im, k_re, k_im = _discretize(ar_ref[0, 0], ai_ref[0, 0], ldt_ref[0, 0])
    btr, bti = btr_ref[0, 0], bti_ref[0, 0]
    ctr, cti = ctr_ref[0, 0], cti_ref[0, 0]
    cr, ci = cr_ref[0, 0], ci_ref[0, 0]
    ctr_hi, ctr_lo = _split_bf16(ctr)
    ncti_hi, ncti_lo = _split_bf16(-cti)

    pw_re = jnp.ones_like(ab_re)
    pw_im = jnp.zeros_like(ab_re)
    for m in range(CHUNK):
        w_re, w_im = _cmul(pw_re, pw_im, k_re, k_im)
        l_re = btr * w_re - bti * w_im
        l_im = btr * w_im + bti * w_re
        dm = _dot3(l_re, ctr_hi, ctr_lo) + _dot3(l_im, ncti_hi, ncti_lo)
        d_ref[0, 0, m] = dm.astype(BF16)
        tp = CHUNK - 1 - m
        bdz_ref[0, 0, tp * LANES:(tp + 1) * LANES, 0:STATE_LANES] = l_re.astype(BF16)
        bdz_ref[0, 0, tp * LANES:(tp + 1) * LANES, STATE_LANES:2 * STATE_LANES] = l_im.astype(BF16)
        pw_re, pw_im = _cmul(pw_re, pw_im, ab_re, ab_im)
        bdh_ref[0, 0, m * LANES:(m + 1) * LANES, 0:STATE_LANES] = (cr * pw_re - ci * pw_im).astype(BF16)
        bdh_ref[0, 0, m * LANES:(m + 1) * LANES, STATE_LANES:2 * STATE_LANES] = (
            -(cr * pw_im + ci * pw_re)).astype(BF16)

    q_re, q_im, _, _ = _discretize(a4r_ref[0, 0], a4i_ref[0, 0], ldt4_ref[0, 0])
    for _ in range(3):
        q_re, q_im = _cmul(q_re, q_im, q_re, q_im)
    ac_ref[0, 0, 8:12, :] = q_re
    ac_ref[0, 0, 12:16, :] = q_im
    q_re, q_im = _cmul(q_re, q_im, q_re, q_im)
    ac_ref[0, 0, 0:4, :] = q_re
    ac_ref[0, 0, 4:8, :] = q_im


def _s5prep_call(ssm_a_re, ssm_a_im, ssm_log_dt, ssm_b_re, ssm_b_im, ssm_c_re, ssm_c_im):
    nb, gb = N_GROUP_BLOCKS, GROUPS_PER_BLOCK
    eye = jnp.eye(gb, dtype=F32)
    row = lambda a: a.reshape(DEPTH, nb, 1, STATE_LANES)
    quad = lambda a: a.reshape(DEPTH, nb, 4, LANES)
    ldt = jnp.broadcast_to(ssm_log_dt[:, :, None], (DEPTH, N_SSM_GROUPS, SSM_STATE))

    def bt_bd(b):
        bt = b.reshape(DEPTH, nb, gb, SSM_STATE, SSM_GROUP).transpose(0, 1, 2, 4, 3)
        return jnp.einsum('lngkp,gh->lngkhp', bt, eye).reshape(DEPTH, nb, LANES, STATE_LANES)

    def ct_bd(c):
        ct = c.reshape(DEPTH, nb, gb, SSM_GROUP, SSM_STATE).transpose(0, 1, 2, 4, 3)
        return jnp.einsum('lngpk,gh->lngphk', ct, eye).reshape(DEPTH, nb, STATE_LANES, LANES)

    def c_bd(c):
        cc = c.reshape(DEPTH, nb, gb, SSM_GROUP, SSM_STATE)
        return jnp.einsum('lngkp,gh->lngkhp', cc, eye).reshape(DEPTH, nb, LANES, STATE_LANES)

    blk = lambda *shape: pl.BlockSpec((1, 1) + shape, lambda l, g: (l, g) + (0,) * len(shape))
    return pl.pallas_call(
        _s5prep_kernel,
        grid=(DEPTH, nb),
        in_specs=[blk(1, STATE_LANES)] * 3 + [blk(4, LANES)] * 3
        + [blk(LANES, STATE_LANES)] * 2 + [blk(STATE_LANES, LANES)] * 2 + [blk(LANES, STATE_LANES)] * 2,
        out_specs=[blk(CHUNK, LANES, LANES), blk(CHUNK * LANES, 2 * STATE_LANES),
                   blk(CHUNK * LANES, 2 * STATE_LANES), blk(16, LANES)],
        out_shape=[
            jax.ShapeDtypeStruct((DEPTH, nb, CHUNK, LANES, LANES), BF16),
            jax.ShapeDtypeStruct((DEPTH, nb, CHUNK * LANES, 2 * STATE_LANES), BF16),
            jax.ShapeDtypeStruct((DEPTH, nb, CHUNK * LANES, 2 * STATE_LANES), BF16),
            jax.ShapeDtypeStruct((DEPTH, nb, 16, LANES), F32),
        ],
        compiler_params=_cp("parallel", "parallel"),
        name="s5_prep",
    )(row(ssm_a_re), row(ssm_a_im), row(ldt), quad(ssm_a_re), quad(ssm_a_im), quad(ldt),
      bt_bd(ssm_b_re), bt_bd(ssm_b_im), ct_bd(ssm_c_re), ct_bd(ssm_c_im), c_bd(ssm_c_re), c_bd(ssm_c_im))


def _gelu_tanh(x):
    return x * (0.5 * (1.0 + jnp.tanh(math.sqrt(2.0 / math.pi) * (x + 0.044715 * (x * x * x)))))


def _s5_kernel(us_ref, h0_ref, d_ref, bdz_ref, bdh_ref, ac_ref, dsk_ref,
               y_ref, ht_ref, bd_scr, ucat_scr, zs_scr, hs_scr, hcat_scr, *, lc, nsq, nc):
    m = nsq * nc
    kd = lc * LANES
    first_b = pl.program_id(1) == 0

    @pl.when(jnp.logical_and(pl.program_id(0) == 0, first_b))
    def _():
        bd_scr[...] = jnp.zeros_like(bd_scr)

    @pl.when(first_b)
    def _():
        for tp in range(lc):
            for t in range(tp, lc):
                bd_scr[tp * LANES:(tp + 1) * LANES, t * LANES:(t + 1) * LANES] = d_ref[0, 0, t - tp]

    for t in range(lc):
        ucat_scr[:, t * LANES:(t + 1) * LANES] = us_ref[pl.ds(t, m, stride=lc), :].astype(BF16)
    z = _dot(ucat_scr[...], bdz_ref[0, 0, (CHUNK - lc) * LANES:CHUNK * LANES, :])
    for s in range(8):
        zs_scr[:, s, :] = z[:, s * LANES:(s + 1) * LANES]

    r0 = 0 if lc == CHUNK else 8
    a_re = ac_ref[0, 0, r0:r0 + 4, :]
    a_im = ac_ref[0, 0, r0 + 4:r0 + 8, :]
    if nc == 1:
        h = h0_ref[:, 0]
        hs_scr[...] = h
        h_re, h_im = h[:, 0:4, :], h[:, 4:8, :]
        zz = zs_scr[...]
        ht_ref[:, 0, 0:4, :] = a_re * h_re - a_im * h_im + zz[:, 0:4, :]
        ht_ref[:, 0, 4:8, :] = a_re * h_im + a_im * h_re + zz[:, 4:8, :]
    else:
        def step(c, carry):
            new = []
            for q in range(nsq):
                h_re, h_im = carry[2 * q], carry[2 * q + 1]
                r = q * nc + c
                hs_scr[r, 0:4, :] = h_re
                hs_scr[r, 4:8, :] = h_im
                zz = zs_scr[r]
                new.append(a_re * h_re - a_im * h_im + zz[0:4, :])
                new.append(a_re * h_im + a_im * h_re + zz[4:8, :])
            return tuple(new)

        init = tuple(h0_ref[q, 0, r4:r4 + 4, :] for q in range(nsq) for r4 in (0, 4))
        fin = lax.fori_loop(0, nc, step, init)
        for q in range(nsq):
            ht_ref[q, 0, 0:4, :] = fin[2 * q]
            ht_ref[q, 0, 4:8, :] = fin[2 * q + 1]

    for s in range(8):
        hcat_scr[:, s * LANES:(s + 1) * LANES] = hs_scr[:, s, :].astype(BF16)
    y = _dot(ucat_scr[...], bd_scr[0:kd, 0:kd]) + lax.dot_general(
        hcat_scr[...], bdh_ref[0, 0, 0:kd, :], (((1,), (1,)), ((), ())), preferred_element_type=F32)
    for t in range(lc):
        u = us_ref[pl.ds(t, m, stride=lc), :]
        v = y[:, t * LANES:(t + 1) * LANES] + dsk_ref[...] * u
        y_ref[pl.ds(t, m, stride=lc), :] = _gelu_tanh(v).astype(y_ref.dtype)


def _s5_call(us, h0, dmat, bdz, bdh, acoef, dskip, layer, nseq, s):
    if s % CHUNK == 0:
        lc, nsq, nc = CHUNK, nseq, s // CHUNK
    else:
        lc, nsq, nc = s, nseq, 1
    nb_seq = nseq // nsq
    m = nsq * nc
    rows = m * lc
    mode = dict(pipeline_mode=pl.Buffered(1)) if m > LANES else {}
    lay = lambda *shape: pl.BlockSpec((1, 1) + shape, lambda g, b: (layer, g) + (0,) * len(shape), **mode)
    kern = functools.partial(_s5_kernel, lc=lc, nsq=nsq, nc=nc)
    return pl.pallas_call(
        kern,
        grid=(N_GROUP_BLOCKS, nb_seq),
        in_specs=[
            pl.BlockSpec((rows, LANES), lambda g, b: (b, g)),
            pl.BlockSpec((nsq, 1, 8, LANES), lambda g, b: (b, g, 0, 0)),
            lay(CHUNK, LANES, LANES),
            lay(CHUNK * LANES, 2 * STATE_LANES),
            lay(CHUNK * LANES, 2 * STATE_LANES),
            lay(16, LANES),
            pl.BlockSpec((1, LANES), lambda g, b: (0, g)),
        ],
        out_specs=[
            pl.BlockSpec((rows, LANES), lambda g, b: (b, g)),
            pl.BlockSpec((nsq, 1, 8, LANES), lambda g, b: (b, g, 0, 0)),
        ],
        out_shape=[
            jax.ShapeDtypeStruct((nseq * s, D_SSM), F32),
            jax.ShapeDtypeStruct((nseq, N_GROUP_BLOCKS, 8, LANES), F32),
        ],
        scratch_shapes=[
            pltpu.VMEM((CHUNK * LANES, CHUNK * LANES), BF16),
            pltpu.VMEM((m, lc * LANES), BF16),
            pltpu.VMEM((m, 8, LANES), F32),
            pltpu.VMEM((m, 8, LANES), F32),
            pltpu.VMEM((m, 2 * STATE_LANES), BF16),
        ],
        compiler_params=_cp("arbitrary", "arbitrary"),
        name="s5_scan",
    )(us, h0, dmat, bdz, bdh, acoef, dskip)


def _state_to_blocks(h_re, h_im):
    nseq = h_re.shape[0]
    q = lambda a: a.reshape(nseq, N_GROUP_BLOCKS, 4, LANES)
    return jnp.concatenate([q(h_re), q(h_im)], axis=2)


def _blocks_to_state(hb):
    nseq = hb.shape[0]
    return (hb[:, :, 0:4, :].reshape(nseq, N_SSM_GROUPS, SSM_STATE),
            hb[:, :, 4:8, :].reshape(nseq, N_SSM_GROUPS, SSM_STATE))


HALO = CONV_WIDTH - 1
PAD0 = 32 - HALO


def _conv_kernel(u_ref, st_ref, w_ref, b_ref, g_ref, beta_ref, v_ref, nst_ref, xp_scr, cv_scr, sh_scr, *, rr):
    bs, ts, c = u_ref.shape
    first = pl.program_id(1) == 0
    for b in range(bs):
        @pl.when(first)
        def _():
            xp_scr[b, PAD0:32, :] = st_ref[b]

        xp_scr[b, 32:32 + ts, :] = u_ref[b]

        def body(r, carry):
            base = pl.multiple_of(r * rr, rr)
            for c0 in range(0, c, LANES):
                win = xp_scr[b, pl.ds(base, rr + 32), c0:c0 + LANES]
                acc = jnp.zeros((rr, LANES), F32)
                for ph in range(8):
                    offs = [PAD0 + k for k in range(CONV_WIDTH) if (PAD0 + k) % 8 == ph]
                    span = max(offs) - ph + rr
                    sh_scr[ph, 0:span, :] = win[ph:ph + span]
                    for o in offs:
                        a = o - ph
                        acc = acc + w_ref[o - PAD0:o - PAD0 + 1, c0:c0 + LANES] * sh_scr[ph, a:a + rr, :]
                cv_scr[:, c0:c0 + LANES] = acc
            y = _ln(cv_scr[...] + b_ref[...]) * g_ref[...] + beta_ref[...]
            v_ref[b, pl.ds(base, rr), :] = (y * _sigmoid(y)).astype(v_ref.dtype)
            return carry

        lax.fori_loop(0, ts // rr, body, 0)
        tail = xp_scr[b, ts + PAD0:ts + 32, :]
        nst_ref[b] = tail
        xp_scr[b, PAD0:32, :] = tail


def _conv_call(uc, state, conv_w, conv_b, ln_g, ln_b, bs, ts):
    nseq, s, c = uc.shape
    rr = min(ts, 64)
    vec = lambda a: a.reshape(1, c)
    cst = lambda shape: pl.BlockSpec(shape, lambda i, j: (0,) * len(shape))
    return pl.pallas_call(
        functools.partial(_conv_kernel, rr=rr),
        grid=(nseq // bs, s // ts),
        in_specs=[
            pl.BlockSpec((bs, ts, c), lambda i, j: (i, j, 0)),
            pl.BlockSpec((bs, HALO, c), lambda i, j: (i, 0, 0)),
            cst((CONV_WIDTH, c)), cst((1, c)), cst((1, c)), cst((1, c)),
        ],
        out_specs=[
            pl.BlockSpec((bs, ts, c), lambda i, j: (i, j, 0)),
            pl.BlockSpec((bs, HALO, c), lambda i, j: (i, 0, 0)),
        ],
        out_shape=[
            jax.ShapeDtypeStruct((nseq, s, c), F32),
            jax.ShapeDtypeStruct((nseq, HALO, c), F32),
        ],
        scratch_shapes=[pltpu.VMEM((bs, ts + 32, c), F32), pltpu.VMEM((rr, c), F32),
                        pltpu.VMEM((8, rr + 32, LANES), F32)],
        compiler_params=_cp("parallel", "arbitrary"),
        name="conv_module",
    )(uc, state, conv_w, vec(conv_b), vec(ln_g), vec(ln_b))


def _route(logits):
    lane = lax.broadcasted_iota(jnp.int32, logits.shape, 1).astype(F32)
    neg = jnp.float32(-jnp.inf)
    big = jnp.float32(1e9)
    gmask = jnp.logical_and(lane >= N_EXPERTS, lane < N_EXPERTS + N_EXPERT_GROUPS)
    gmax = jnp.max(jnp.where(gmask, logits, neg), axis=-1, keepdims=True)
    gsel = jnp.min(jnp.where(jnp.logical_and(gmask, logits == gmax), lane, big), axis=-1, keepdims=True)
    p_g = 1.0 / jnp.sum(jnp.where(gmask, jnp.exp(logits - gmax), 0.0), axis=-1, keepdims=True)
    lo = (gsel - N_EXPERTS) * EXPERTS_PER_GROUP
    emask = jnp.logical_and(lane >= lo, lane < lo + EXPERTS_PER_GROUP)
    v1 = jnp.max(jnp.where(emask, logits, neg), axis=-1, keepdims=True)
    i1 = jnp.min(jnp.where(jnp.logical_and(emask, logits == v1), lane, big), axis=-1, keepdims=True)
    emask2 = jnp.logical_and(emask, lane != i1)
    v2 = jnp.max(jnp.where(emask2, logits, neg), axis=-1, keepdims=True)
    i2 = jnp.min(jnp.where(jnp.logical_and(emask2, logits == v2), lane, big), axis=-1, keepdims=True)
    e2 = jnp.exp(v2 - v1)
    den = 1.0 + e2
    w1 = (1.0 / den) * p_g
    w2 = (e2 / den) * p_g
    comb = jnp.where(lane == i1, w1, 0.0) + jnp.where(lane == i2, w2, 0.0)
    return comb + jnp.where(lane == ROUTE_GROUP_LANE, gsel - N_EXPERTS, 0.0)


def _to_row_tiles(ref, x, rows=slice(None)):
    for s in range(x.shape[-1] // LANES):
        ref[rows, s, :] = x[:, s * LANES:(s + 1) * LANES]


def _from_row_tiles(ref, scr):
    for s in range(ref.shape[1]):
        scr[:, s * LANES:(s + 1) * LANES] = ref[:, s, :].astype(scr.dtype)


def _mix_kernel(ys_ref, v_ref, gs_ref, gc_ref, x_ref, g1_ref, sc2_ref, sh2_ref,
                wv_ref, wg_ref, wpw_ref, wo_ref, l1g_ref, l1b_ref, wrh_ref, wrl_ref, br_ref,
                x1_ref, h2_ref, comb_ref):
    bs, ts, d = x_ref.shape
    mh = bs * ts // 2
    for k in range(2):
        if bs > 1:
            bsl, tsl = slice(k * bs // 2, (k + 1) * bs // 2), slice(0, ts)
        else:
            bsl, tsl = slice(0, bs), slice(k * ts // 2, (k + 1) * ts // 2)
        rows = slice(k * mh, (k + 1) * mh)
        ys = ys_ref[rows, :].astype(BF16)
        br_s = _dot(ys, wv_ref[...]) * _sigmoid(_dot(ys, wg_ref[...]))
        br_c = _dot(v_ref[rows, :].astype(BF16), wpw_ref[...])
        mix = gs_ref[rows, :].astype(F32) * br_s + gc_ref[rows, :].astype(F32) * br_c
        x = x_ref[bsl, tsl, :]
        o = _dot(mix.astype(BF16), wo_ref[...]).reshape(x.shape)
        x1 = _ln(ALPHA * x + g1_ref[bsl] * o) * l1g_ref[...] + l1b_ref[...]
        x1_ref[bsl, tsl, :] = x1
        h2 = (_ln(x1) * (1.0 + sc2_ref[bsl]) + sh2_ref[bsl]).reshape(mh, d)
        _to_row_tiles(h2_ref, h2, rows)
        comb_ref[rows, :] = _route(_dot3(h2, wrh_ref[...], wrl_ref[...]) + br_ref[...])


N_MIX_INPUTS = 17


def _mix_kernel_into(*refs):
    _mix_kernel(*refs[:N_MIX_INPUTS], *refs[N_MIX_INPUTS + 1:])


def _mix_call(ys, v, g, x, g1, sc2, sh2, w_val, w_gate, w_pw, w_out, ln_g, ln_b, w_r_hi, w_r_lo, b_r,
              bs, ts, t_all, tok0, h2_buf):
    nseq, s, d = x.shape
    ni, nj = nseq // bs, s // ts
    m = bs * ts
    t = nseq * s
    blk0 = tok0 // m
    row0 = lambda i, j: (i * nj + j, 0)
    mod = pl.BlockSpec((bs, 1, d), lambda i, j: (i, 0, 0))
    res = lambda shape: pl.BlockSpec(shape, lambda i, j: (0,) * len(shape), pipeline_mode=pl.Buffered(1))
    return pl.pallas_call(
        _mix_kernel if h2_buf is None else _mix_kernel_into,
        grid=(ni, nj),
        in_specs=[
            pl.BlockSpec((m, D_SSM), row0),
            pl.BlockSpec((m, D_CONV), row0),
            pl.BlockSpec((m, d), row0),
            pl.BlockSpec((m, d), lambda i, j: (i * nj + j, 1)),
            pl.BlockSpec((bs, ts, d), lambda i, j: (i, j, 0)),
            mod, mod, mod,
            res((D_SSM, d)), res((D_SSM, d)), res((D_CONV, d)), res((d, d)),
            res((1, d)), res((1, d)), res((d, LANES)), res((d, LANES)), res((1, LANES)),
        ] + ([] if h2_buf is None else [pl.BlockSpec(memory_space=pl.ANY)]),
        out_specs=[
            pl.BlockSpec((bs, ts, d), lambda i, j: (i, j, 0)),
            pl.BlockSpec((m, d // LANES, LANES), lambda i, j: (blk0 + i * nj + j, 0, 0)),
            pl.BlockSpec((m, LANES), row0),
        ],
        out_shape=[
            jax.ShapeDtypeStruct((nseq, s, d), F32),
            jax.ShapeDtypeStruct((t_all, d // LANES, LANES), F32),
            jax.ShapeDtypeStruct((t, LANES), F32),
        ],
        input_output_aliases={} if h2_buf is None else {N_MIX_INPUTS: 1},
        compiler_params=_cp("parallel", "parallel"),
        name="branch_mix",
    )(ys, v, g, g, x, g1, sc2, sh2, w_val, w_gate, w_pw, w_out,
      ln_g.reshape(1, d), ln_b.reshape(1, d), w_r_hi, w_r_lo, b_r, *([] if h2_buf is None else [h2_buf]))


def _moe_plan(gid, tm):
    t = gid.shape[0]
    npad = t + N_EXPERT_GROUPS * tm
    onehot = (gid[:, None] == jnp.arange(N_EXPERT_GROUPS, dtype=jnp.int32)[None, :]).astype(jnp.int32)
    csum = jnp.cumsum(onehot, axis=0)
    rank = jnp.sum(csum * onehot, axis=1) - 1
    cnt = csum[-1]
    seg = ((cnt + tm - 1) // tm) * tm
    end = jnp.cumsum(seg)
    pos = (end - seg)[gid] + rank
    src = jnp.zeros((npad,), jnp.int32).at[pos].set(jnp.arange(t, dtype=jnp.int32), unique_indices=True)
    tile_start = jnp.arange(npad // tm, dtype=jnp.int32) * tm
    tile_group = jnp.minimum(jnp.sum((tile_start[:, None] >= end[None, :]).astype(jnp.int32), axis=1),
                             N_EXPERT_GROUPS - 1)
    n_tiles = (end[-1] // tm).reshape(1)
    return pos.astype(jnp.int32), src, tile_group.astype(jnp.int32), n_tiles.astype(jnp.int32)


def _gather_kernel(idx_ref, tab_ref, out_ref):
    rows = out_ref.shape[0]
    base = pl.program_id(1) * rows

    def body(r, carry):
        out_ref[r] = tab_ref[idx_ref[base + r]]
        return carry

    lax.fori_loop(0, rows, body, 0, unroll=8)


def _gather_call(table, idx, rows):
    n, nt, lanes = table.shape
    r_total = idx.shape[0]
    half = 8
    return pl.pallas_call(
        _gather_kernel,
        grid_spec=pltpu.PrefetchScalarGridSpec(
            num_scalar_prefetch=1,
            grid=(nt // half, r_total // rows),
            in_specs=[pl.BlockSpec((n, half, lanes), lambda h, i, idx_ref: (0, h, 0),
                                   pipeline_mode=pl.Buffered(1))],
            out_specs=pl.BlockSpec((rows, half, lanes), lambda h, i, idx_ref: (i, h, 0)),
        ),
        out_shape=jax.ShapeDtypeStruct((r_total, nt, lanes), table.dtype),
        compiler_params=_cp("arbitrary", "arbitrary"),
        name="row_gather",
    )(idx, table)


def _experts_kernel(tg_ref, nt_ref, xs_ref, cs_ref, wu_ref, wg_ref, wd_ref, ys_ref, h_scr, act_scr):
    i = pl.program_id(0)
    live = i < nt_ref[0]

    @pl.when(live)
    def _():
        _from_row_tiles(xs_ref, h_scr)
        h = h_scr[...]
        cs = cs_ref[...]
        lane = lax.broadcasted_iota(jnp.int32, cs.shape, 1)
        e0 = tg_ref[i] * EXPERTS_PER_GROUP
        for j in range(EXPERTS_PER_GROUP):
            cols = slice(j * D_EXPERT, (j + 1) * D_EXPERT)
            up = _dot(h, wu_ref[:, cols])
            gt = _dot(h, wg_ref[:, cols])
            ce = jnp.sum(jnp.where(lane == e0 + j, cs, 0.0), axis=-1, keepdims=True)
            act_scr[:, cols] = ((gt * _sigmoid(gt)) * up * ce).astype(BF16)
        _to_row_tiles(ys_ref, _dot(act_scr[...], wd_ref[...]))

    @pl.when(jnp.logical_not(live))
    def _():
        ys_ref[...] = jnp.zeros_like(ys_ref)


def _experts_call(xs, cs, tile_group, n_tiles, w_up, w_gate, w_down, tm):
    npad, nt, lanes = xs.shape
    d = nt * lanes
    gf = EXPERTS_PER_GROUP * D_EXPERT
    one = pl.Buffered(1)
    return pl.pallas_call(
        _experts_kernel,
        grid_spec=pltpu.PrefetchScalarGridSpec(
            num_scalar_prefetch=2,
            grid=(npad // tm,),
            in_specs=[
                pl.BlockSpec((tm, nt, lanes), lambda i, tg, n: (i, 0, 0)),
                pl.BlockSpec((tm, LANES), lambda i, tg, n: (i, 0)),
                pl.BlockSpec((d, gf), lambda i, tg, n: (0, tg[i]), pipeline_mode=one),
                pl.BlockSpec((d, gf), lambda i, tg, n: (0, tg[i]), pipeline_mode=one),
                pl.BlockSpec((gf, d), lambda i, tg, n: (tg[i], 0), pipeline_mode=one),
            ],
            out_specs=pl.BlockSpec((tm, nt, lanes), lambda i, tg, n: (i, 0, 0)),
            scratch_shapes=[pltpu.VMEM((tm, d), BF16), pltpu.VMEM((tm, gf), BF16)],
        ),
        out_shape=jax.ShapeDtypeStruct((npad, nt, lanes), F32),
        compiler_params=_cp("arbitrary"),
        name="moe_experts",
    )(tile_group, n_tiles, xs, cs, w_up, w_gate, w_down)


def _final_kernel(ff_ref, x1_ref, g2_ref, l2g_ref, l2b_ref, x2_ref, ff_scr):
    bs, ts, d = x1_ref.shape
    _from_row_tiles(ff_ref, ff_scr)
    ff = ff_scr[...].reshape(bs, ts, d)
    x2_ref[...] = _ln(ALPHA * x1_ref[...] + g2_ref[...] * ff) * l2g_ref[...] + l2b_ref[...]


def _final_call(ff, x1, g2, ln_g, ln_b, tok0, bs, ts):
    nseq, s, d = x1.shape
    ni, nj = nseq // bs, s // ts
    m = bs * ts
    t0 = tok0 // m
    vec = pl.BlockSpec((1, d), lambda i, j: (0, 0))
    return pl.pallas_call(
        _final_kernel,
        grid=(ni, nj),
        in_specs=[
            pl.BlockSpec((m, d // LANES, LANES), lambda i, j: (t0 + i * nj + j, 0, 0)),
            pl.BlockSpec((bs, ts, d), lambda i, j: (i, j, 0)),
            pl.BlockSpec((bs, 1, d), lambda i, j: (i, 0, 0)),
            vec, vec,
        ],
        out_specs=pl.BlockSpec((bs, ts, d), lambda i, j: (i, j, 0)),
        out_shape=jax.ShapeDtypeStruct((nseq, s, d), F32),
        scratch_shapes=[pltpu.VMEM((m, d), F32)],
        compiler_params=_cp("parallel", "parallel"),
        name="moe_residual_ln",
    )(ff, x1, g2, ln_g.reshape(1, d), ln_b.reshape(1, d))


def kernel(x_prompt, x_sample, c_prompt, c_sample, state_ssm_re, state_ssm_im, state_conv, w_ada, b_ada, w_in, ssm_a_re, ssm_a_im, ssm_log_dt, ssm_b_re, ssm_b_im, ssm_c_re, ssm_c_im, ssm_d, w_s5_val, w_s5_gate, conv_w, conv_b, conv_ln_g, conv_ln_b, w_conv_pw, w_out, ln1_g, ln1_b, moe_w_group, moe_b_group, moe_w_router, moe_b_router, moe_w_up, moe_w_gate, moe_w_down, ln2_g, ln2_b):
    n_p, s_p = x_prompt.shape[0], x_prompt.shape[1]
    n_s, s_s = x_sample.shape[0], x_sample.shape[1]

    pad_rows = (-(n_p + n_s)) % 8
    c_all = jnp.concatenate([c_prompt, c_sample, jnp.zeros((pad_rows, D_MODEL), F32)], axis=0)
    ada = _ada_call(c_all, w_ada, b_ada)
    dmat, bdz, bdh, acoef = _s5prep_call(ssm_a_re, ssm_a_im, ssm_log_dt, ssm_b_re, ssm_b_im,
                                         ssm_c_re, ssm_c_im)

    w_inb = _cast_call(w_in, BF16)
    w_val = _cast_call(w_s5_val, BF16)
    w_gate = _cast_call(w_s5_gate, BF16)
    w_pw = _cast_call(w_conv_pw, BF16)
    w_o = _cast_call(w_out, BF16)
    ef = N_EXPERTS * D_EXPERT
    w_up = _cast_experts_call(moe_w_up, BF16)
    w_gt = _cast_experts_call(moe_w_gate, BF16)
    w_dn = _cast_call(moe_w_down.reshape(DEPTH, ef, D_MODEL), BF16)
    zpad = LANES - N_EXPERTS - N_EXPERT_GROUPS
    w_r = jnp.concatenate([moe_w_router, moe_w_group, jnp.zeros((DEPTH, D_MODEL, zpad), F32)], axis=-1)
    w_r_hi = w_r.astype(BF16)
    w_r_lo = (w_r - w_r_hi.astype(F32)).astype(BF16)
    b_r = jnp.concatenate([moe_b_router, moe_b_group, jnp.zeros((DEPTH, zpad), F32)], axis=-1)
    b_r = b_r.reshape(DEPTH, 1, LANES)
    dskip = ssm_d.reshape(DEPTH, 1, D_SSM)

    streams = [
        dict(x=x_prompt, rows=slice(0, n_p), n=n_p, s=s_p, bs=1, ts=512, mbs=1, mts=256,
             h0=jnp.zeros((n_p, N_GROUP_BLOCKS, 8, LANES), F32),
             conv=jnp.zeros((n_p, HALO, D_CONV), F32), cbs=1, cts=512, per_layer_state=False),
        dict(x=x_sample, rows=slice(n_p, n_p + n_s), n=n_s, s=s_s, bs=64, ts=s_s, mbs=32, mts=s_s,
             cbs=8, cts=s_s, per_layer_state=True),
    ]
    outs = [dict(re=[], im=[], conv=[]) for _ in streams]

    t_all = sum(st['n'] * st['s'] for st in streams)
    for l in range(DEPTH):
        mids = []
        h2_all, tok0 = None, 0
        for st, out in zip(streams, outs):
            n, s, bs, ts = st['n'], st['s'], st['bs'], st['ts']
            x = st['x']
            mod = lambda k: ada[l, st['rows'], k * D_MODEL:(k + 1) * D_MODEL].reshape(n, 1, D_MODEL)
            sh1, sc1, g1, sh2, sc2, g2 = (mod(k) for k in range(ADA_CHUNKS))
            if st['per_layer_state']:
                h0 = _state_to_blocks(state_ssm_re[l], state_ssm_im[l])
                cst = state_conv[l]
            else:
                h0, cst = st['h0'], st['conv']

            us, uc, g = _inproj_call(x, sc1, sh1, w_inb[l], bs, ts)
            ys, ht = _s5_call(us, h0, dmat, bdz, bdh, acoef, dskip[l], l, n, s)
            v, nbuf = _conv_call(uc.reshape(n, s, D_CONV), cst, conv_w[l], conv_b[l],
                                 conv_ln_g[l], conv_ln_b[l], st['cbs'], st['cts'])
            x1, h2_all, comb = _mix_call(ys, v.reshape(n * s, D_CONV), g, x, g1, sc2, sh2,
                                         w_val[l], w_gate[l], w_pw[l], w_o[l], ln1_g[l], ln1_b[l],
                                         w_r_hi[l], w_r_lo[l], b_r[l], st['mbs'], st['mts'],
                                         t_all, tok0, h2_all)
            tok0 += n * s
            mids.append((x1, g2, comb))
            h_re, h_im = _blocks_to_state(ht)
            out['re'].append(h_re)
            out['im'].append(h_im)
            out['conv'].append(nbuf)

        comb_all = jnp.concatenate([m_[2] for m_ in mids], axis=0)
        gid = comb_all[:, ROUTE_GROUP_LANE].astype(jnp.int32)
        pos, src, tile_group, n_tiles = _moe_plan(gid, MOE_TILE)
        xs = _gather_call(h2_all, src, GATHER_ROWS)
        ysort = _experts_call(xs, comb_all[src], tile_group, n_tiles, w_up[l], w_gt[l], w_dn[l], MOE_TILE)
        ff = _gather_call(ysort, pos, GATHER_ROWS)
        tok0 = 0
        for st, (x1, g2, _) in zip(streams, mids):
            st['x'] = _final_call(ff, x1, g2, ln2_g[l], ln2_b[l], tok0, st['bs'], st['ts'])
            tok0 += st['n'] * st['s']

    stack = lambda xs: jnp.stack(xs, axis=0)
    po, so = outs
    return (streams[0]['x'], streams[1]['x'],
            stack(po['re']), stack(po['im']), stack(po['conv']),
            stack(so['re']), stack(so['im']), stack(so['conv']))
```

```python
import functools
import math

import jax
import jax.numpy as jnp
from jax import lax
from jax.experimental import pallas as pl
from jax.experimental.pallas import tpu as pltpu

F32 = jnp.float32
BF16 = jnp.bfloat16

D_MODEL = 2048
DEPTH = 4
D_SSM = 1024
SSM_GROUP = 16
N_SSM_GROUPS = 64
SSM_STATE = 64
D_CONV = 1024
CONV_WIDTH = 31
N_EXPERT_GROUPS = 4
EXPERTS_PER_GROUP = 8
N_EXPERTS = 32
D_EXPERT = 256
ALPHA = (2 * DEPTH) ** 0.25
LN_EPS = 1e-5
ADA_CHUNKS = 6

LANES = 128
CHUNK = 16
GROUPS_PER_BLOCK = LANES // SSM_GROUP
N_GROUP_BLOCKS = N_SSM_GROUPS // GROUPS_PER_BLOCK
STATE_LANES = GROUPS_PER_BLOCK * SSM_STATE
VMEM_LIMIT = 56 * 1024 * 1024
ROUTE_GROUP_LANE = N_EXPERTS
MOE_TILE = 256
GATHER_ROWS = 512


def _cp(*sem):
    return pltpu.CompilerParams(dimension_semantics=sem, vmem_limit_bytes=VMEM_LIMIT)


def _ln(x):
    mu = jnp.mean(x, axis=-1, keepdims=True)
    xc = x - mu
    var = jnp.mean(xc * xc, axis=-1, keepdims=True)
    return xc * lax.rsqrt(var + LN_EPS)


def _sigmoid(x):
    return jax.nn.sigmoid(x)


def _dot(a, b):
    return jnp.dot(a, b, preferred_element_type=F32)


def _split_bf16(x):
    hi = x.astype(BF16)
    return hi, (x - hi.astype(F32)).astype(BF16)


def _dot3(a, b_hi, b_lo):
    a_hi, a_lo = _split_bf16(a)
    return _dot(a_hi, b_hi) + _dot(a_hi, b_lo) + _dot(a_lo, b_hi)


CAST_BLOCK_BYTES = 8 * 1024 * 1024


def _cast_kernel(x_ref, o_ref):
    o_ref[...] = x_ref[...].astype(o_ref.dtype)


def _cast_call(x, dtype):
    nl, r, c = x.shape
    rb = min(r, 2 ** int(math.log2(CAST_BLOCK_BYTES // (c * x.dtype.itemsize))))
    blk = pl.BlockSpec((1, rb, c), lambda l, i: (l, i, 0))
    return pl.pallas_call(
        _cast_kernel, grid=(nl, r // rb), in_specs=[blk], out_specs=blk,
        out_shape=jax.ShapeDtypeStruct(x.shape, dtype),
        compiler_params=_cp("parallel", "parallel"), name="cast",
    )(x)


def _cast_experts_kernel(x_ref, o_ref):
    for j in range(x_ref.shape[1]):
        f = x_ref.shape[3]
        o_ref[0, :, j * f:(j + 1) * f] = x_ref[0, j].astype(o_ref.dtype)


def _cast_experts_call(w, dtype):
    nl, ne, d, f = w.shape
    eb = min(ne, CAST_BLOCK_BYTES // (d * f * w.dtype.itemsize))
    return pl.pallas_call(
        _cast_experts_kernel,
        grid=(nl, ne // eb),
        in_specs=[pl.BlockSpec((1, eb, d, f), lambda l, e: (l, e, 0, 0))],
        out_specs=pl.BlockSpec((1, d, eb * f), lambda l, e: (l, 0, e)),
        out_shape=jax.ShapeDtypeStruct((nl, d, ne * f), dtype),
        compiler_params=_cp("parallel", "parallel"), name="cast_experts",
    )(w)


def _ada_kernel(c_ref, w_ref, b_ref, o_ref):
    c = c_ref[...]
    s = (c * _sigmoid(c)).astype(BF16)
    o_ref[0] = _dot(s, w_ref[0].astype(BF16)) + b_ref[0]


def _ada_call(c_all, w_ada, b_ada):
    rows = c_all.shape[0]
    n_out = ADA_CHUNKS * D_MODEL
    tn = 1024
    return pl.pallas_call(
        _ada_kernel,
        grid=(DEPTH, n_out // tn),
        in_specs=[
            pl.BlockSpec((rows, D_MODEL), lambda l, n: (0, 0)),
            pl.BlockSpec((1, D_MODEL, tn), lambda l, n: (l, 0, n)),
            pl.BlockSpec((1, 1, tn), lambda l, n: (l, 0, n)),
        ],
        out_specs=pl.BlockSpec((1, rows, tn), lambda l, n: (l, 0, n)),
        out_shape=jax.ShapeDtypeStruct((DEPTH, rows, n_out), F32),
        compiler_params=_cp("parallel", "parallel"),
        name="ada_proj",
    )(c_all, w_ada, b_ada.reshape(DEPTH, 1, n_out))


def _inproj_kernel(x_ref, sc_ref, sh_ref, wus_ref, wca_ref, wcb_ref, wg_ref,
                   us_ref, uc_ref, g_ref, h_scr):
    bs, ts, d = x_ref.shape

    @pl.when(pl.program_id(2) == 0)
    def _():
        h = _ln(x_ref[...]) * (1.0 + sc_ref[...]) + sh_ref[...]
        h_scr[...] = h.reshape(bs * ts, d).astype(BF16)

    h = h_scr[...]
    us_ref[...] = _dot(h, wus_ref[...])
    uc_ref[...] = _dot(h, wca_ref[...]) * _sigmoid(_dot(h, wcb_ref[...]))
    g_ref[...] = _sigmoid(_dot(h, wg_ref[...])).astype(g_ref.dtype)


def _inproj_call(x, sc, sh, w_in, layer, bs, ts):
    nseq, s, d = x.shape
    ni, nj, nn = nseq // bs, s // ts, 4
    m = bs * ts
    t = nseq * s
    cn = D_SSM // nn
    gn = 2 * D_MODEL // nn
    g0 = (D_SSM + 2 * D_CONV) // gn
    row = lambda i, j, n: (i * nj + j, n)
    return pl.pallas_call(
        _inproj_kernel,
        grid=(ni, nj, nn),
        in_specs=[
            pl.BlockSpec((bs, ts, d), lambda i, j, n: (i, j, 0)),
            pl.BlockSpec((bs, 1, d), lambda i, j, n: (i, 0, 0)),
            pl.BlockSpec((bs, 1, d), lambda i, j, n: (i, 0, 0)),
            pl.BlockSpec((None, d, cn), lambda i, j, n: (layer, 0, n)),
            pl.BlockSpec((None, d, cn), lambda i, j, n: (layer, 0, nn + n)),
            pl.BlockSpec((None, d, cn), lambda i, j, n: (layer, 0, 2 * nn + n)),
            pl.BlockSpec((None, d, gn), lambda i, j, n: (layer, 0, g0 + n)),
        ],
        out_specs=[
            pl.BlockSpec((m, cn), row),
            pl.BlockSpec((m, cn), row),
            pl.BlockSpec((m, gn), row),
        ],
        out_shape=[
            jax.ShapeDtypeStruct((t, D_SSM), F32),
            jax.ShapeDtypeStruct((t, D_CONV), F32),
            jax.ShapeDtypeStruct((t, 2 * D_MODEL), BF16),
        ],
        scratch_shapes=[pltpu.VMEM((m, d), BF16)],
        compiler_params=_cp("parallel", "parallel", "arbitrary"),
        name="in_proj",
    )(x, sc, sh, w_in, w_in, w_in, w_in)


def _cmul(ar, ai, br, bi):
    return ar * br - ai * bi, ar * bi + ai * br


def _discretize(a_re, a_im, log_dt):
    dt = jnp.exp(log_dt)
    mag = jnp.exp(dt * a_re)
    ang = dt * a_im
    ab_re = mag * jnp.cos(ang)
    ab_im = mag * jnp.sin(ang)
    den = a_re * a_re + a_im * a_im
    k_re = ((ab_re - 1.0) * a_re + ab_im * a_im) / den
    k_im = (ab_im * a_re - (ab_re - 1.0) * a_im) / den
    return ab_re, ab_im, k_re, k_im


def _s5prep_kernel(ar_ref, ai_ref, ldt_ref, a4r_ref, a4i_ref, ldt4_ref,
                   btr_ref, bti_ref, ctr_ref, cti_ref, cr_ref, ci_ref,
                   d_ref, bdz_ref, bdh_ref, ac_ref):
    ab_re, ab_im, k_re, k_im = _discretize(ar_ref[0, 0], ai_ref[0, 0], ldt_ref[0, 0])
    btr, bti = btr_ref[0, 0], bti_ref[0, 0]
    ctr, cti = ctr_ref[0, 0], cti_ref[0, 0]
    cr, ci = cr_ref[0, 0], ci_ref[0, 0]
    ctr_hi, ctr_lo = _split_bf16(ctr)
    ncti_hi, ncti_lo = _split_bf16(-cti)

    pw_re = jnp.ones_like(ab_re)
    pw_im = jnp.zeros_like(ab_re)
    for m in range(CHUNK):
        w_re, w_im = _cmul(pw_re, pw_im, k_re, k_im)
        l_re = btr * w_re - bti * w_im
        l_im = btr * w_im + bti * w_re
        dm = _dot3(l_re, ctr_hi, ctr_lo) + _dot3(l_im, ncti_hi, ncti_lo)
        d_ref[0, 0, m] = dm.astype(BF16)
        tp = CHUNK - 1 - m
        bdz_ref[0, 0, tp * LANES:(tp + 1) * LANES, 0:STATE_LANES] = l_re.astype(BF16)
        bdz_ref[0, 0, tp * LANES:(tp + 1) * LANES, STATE_LANES:2 * STATE_LANES] = l_im.astype(BF16)
        pw_re, pw_im = _cmul(pw_re, pw_im, ab_re, ab_im)
        bdh_ref[0, 0, m * LANES:(m + 1) * LANES, 0:STATE_LANES] = (cr * pw_re - ci * pw_im).astype(BF16)
        bdh_ref[0, 0, m * LANES:(m + 1) * LANES, STATE_LANES:2 * STATE_LANES] = (
            -(cr * pw_im + ci * pw_re)).astype(BF16)

    q_re, q_im, _, _ = _discretize(a4r_ref[0, 0], a4i_ref[0, 0], ldt4_ref[0, 0])
    for _ in range(3):
        q_re, q_im = _cmul(q_re, q_im, q_re, q_im)
    ac_ref[0, 0, 8:12, :] = q_re
    ac_ref[0, 0, 12:16, :] = q_im
    q_re, q_im = _cmul(q_re, q_im, q_re, q_im)
    ac_ref[0, 0, 0:4, :] = q_re
    ac_ref[0, 0, 4:8, :] = q_im


def _s5prep_call(ssm_a_re, ssm_a_im, ssm_log_dt, ssm_b_re, ssm_b_im, ssm_c_re, ssm_c_im):
    nb, gb = N_GROUP_BLOCKS, GROUPS_PER_BLOCK
    eye = jnp.eye(gb, dtype=F32)
    row = lambda a: a.reshape(DEPTH, nb, 1, STATE_LANES)
    quad = lambda a: a.reshape(DEPTH, nb, 4, LANES)
    ldt = jnp.broadcast_to(ssm_log_dt[:, :, None], (DEPTH, N_SSM_GROUPS, SSM_STATE))

    def bt_bd(b):
        bt = b.reshape(DEPTH, nb, gb, SSM_STATE, SSM_GROUP).transpose(0, 1, 2, 4, 3)
        return jnp.einsum('lngkp,gh->lngkhp', bt, eye).reshape(DEPTH, nb, LANES, STATE_LANES)

    def ct_bd(c):
        ct = c.reshape(DEPTH, nb, gb, SSM_GROUP, SSM_STATE).transpose(0, 1, 2, 4, 3)
        return jnp.einsum('lngpk,gh->lngphk', ct, eye).reshape(DEPTH, nb, STATE_LANES, LANES)

    def c_bd(c):
        cc = c.reshape(DEPTH, nb, gb, SSM_GROUP, SSM_STATE)
        return jnp.einsum('lngkp,gh->lngkhp', cc, eye).reshape(DEPTH, nb, LANES, STATE_LANES)

    blk = lambda *shape: pl.BlockSpec((1, 1) + shape, lambda l, g: (l, g) + (0,) * len(shape))
    return pl.pallas_call(
        _s5prep_kernel,
        grid=(DEPTH, nb),
        in_specs=[blk(1, STATE_LANES)] * 3 + [blk(4, LANES)] * 3
        + [blk(LANES, STATE_LANES)] * 2 + [blk(STATE_LANES, LANES)] * 2 + [blk(LANES, STATE_LANES)] * 2,
        out_specs=[blk(CHUNK, LANES, LANES), blk(CHUNK * LANES, 2 * STATE_LANES),
                   blk(CHUNK * LANES, 2 * STATE_LANES), blk(16, LANES)],
        out_shape=[
            jax.ShapeDtypeStruct((DEPTH, nb, CHUNK, LANES, LANES), BF16),
            jax.ShapeDtypeStruct((DEPTH, nb, CHUNK * LANES, 2 * STATE_LANES), BF16),
            jax.ShapeDtypeStruct((DEPTH, nb, CHUNK * LANES, 2 * STATE_LANES), BF16),
            jax.ShapeDtypeStruct((DEPTH, nb, 16, LANES), F32),
        ],
        compiler_params=_cp("parallel", "parallel"),
        name="s5_prep",
    )(row(ssm_a_re), row(ssm_a_im), row(ldt), quad(ssm_a_re), quad(ssm_a_im), quad(ldt),
      bt_bd(ssm_b_re), bt_bd(ssm_b_im), ct_bd(ssm_c_re), ct_bd(ssm_c_im), c_bd(ssm_c_re), c_bd(ssm_c_im))


def _gelu_tanh(x):
    return x * (0.5 * (1.0 + jnp.tanh(math.sqrt(2.0 / math.pi) * (x + 0.044715 * (x * x * x)))))


def _s5_kernel(us_ref, h0_ref, d_ref, bdz_ref, bdh_ref, ac_ref, dsk_ref,
               y_ref, ht_ref, bd_scr, ucat_scr, zs_scr, hs_scr, hcat_scr, *, lc, nsq, nc):
    m = nsq * nc
    kd = lc * LANES
    first_b = pl.program_id(1) == 0

    @pl.when(jnp.logical_and(pl.program_id(0) == 0, first_b))
    def _():
        bd_scr[...] = jnp.zeros_like(bd_scr)

    @pl.when(first_b)
    def _():
        for tp in range(lc):
            for t in range(tp, lc):
                bd_scr[tp * LANES:(tp + 1) * LANES, t * LANES:(t + 1) * LANES] = d_ref[0, 0, t - tp]

    for t in range(lc):
        ucat_scr[:, t * LANES:(t + 1) * LANES] = us_ref[pl.ds(t, m, stride=lc), :].astype(BF16)
    z = _dot(ucat_scr[...], bdz_ref[0, 0, (CHUNK - lc) * LANES:CHUNK * LANES, :])
    for s in range(8):
        zs_scr[:, s, :] = z[:, s * LANES:(s + 1) * LANES]

    r0 = 0 if lc == CHUNK else 8
    a_re = ac_ref[0, 0, r0:r0 + 4, :]
    a_im = ac_ref[0, 0, r0 + 4:r0 + 8, :]
    if nc == 1:
        h = h0_ref[:, 0]
        hs_scr[...] = h
        h_re, h_im = h[:, 0:4, :], h[:, 4:8, :]
        zz = zs_scr[...]
        ht_ref[:, 0, 0:4, :] = a_re * h_re - a_im * h_im + zz[:, 0:4, :]
        ht_ref[:, 0, 4:8, :] = a_re * h_im + a_im * h_re + zz[:, 4:8, :]
    else:
        def step(c, carry):
            new = []
            for q in range(nsq):
                h_re, h_im = carry[2 * q], carry[2 * q + 1]
                r = q * nc + c
                hs_scr[r, 0:4, :] = h_re
                hs_scr[r, 4:8, :] = h_im
                zz = zs_scr[r]
                new.append(a_re * h_re - a_im * h_im + zz[0:4, :])
                new.append(a_re * h_im + a_im * h_re + zz[4:8, :])
            return tuple(new)

        init = tuple(h0_ref[q, 0, r4:r4 + 4, :] for q in range(nsq) for r4 in (0, 4))
        fin = lax.fori_loop(0, nc, step, init)
        for q in range(nsq):
            ht_ref[q, 0, 0:4, :] = fin[2 * q]
            ht_ref[q, 0, 4:8, :] = fin[2 * q + 1]

    for s in range(8):
        hcat_scr[:, s * LANES:(s + 1) * LANES] = hs_scr[:, s, :].astype(BF16)
    y = _dot(ucat_scr[...], bd_scr[0:kd, 0:kd]) + lax.dot_general(
        hcat_scr[...], bdh_ref[0, 0, 0:kd, :], (((1,), (1,)), ((), ())), preferred_element_type=F32)
    for t in range(lc):
        u = us_ref[pl.ds(t, m, stride=lc), :]
        v = y[:, t * LANES:(t + 1) * LANES] + dsk_ref[...] * u
        y_ref[pl.ds(t, m, stride=lc), :] = _gelu_tanh(v).astype(y_ref.dtype)


def _s5_call(us, h0, dmat, bdz, bdh, acoef, dskip, layer, nseq, s):
    if s % CHUNK == 0:
        lc, nsq, nc = CHUNK, nseq, s // CHUNK
    else:
        lc, nsq, nc = s, nseq, 1
    nb_seq = nseq // nsq
    m = nsq * nc
    rows = m * lc
    mode = dict(pipeline_mode=pl.Buffered(1)) if m > LANES else {}
    lay = lambda *shape: pl.BlockSpec((1, 1) + shape, lambda g, b: (layer, g) + (0,) * len(shape), **mode)
    kern = functools.partial(_s5_kernel, lc=lc, nsq=nsq, nc=nc)
    return pl.pallas_call(
        kern,
        grid=(N_GROUP_BLOCKS, nb_seq),
        in_specs=[
            pl.BlockSpec((rows, LANES), lambda g, b: (b, g)),
            pl.BlockSpec((nsq, 1, 8, LANES), lambda g, b: (b, g, 0, 0)),
            lay(CHUNK, LANES, LANES),
            lay(CHUNK * LANES, 2 * STATE_LANES),
            lay(CHUNK * LANES, 2 * STATE_LANES),
            lay(16, LANES),
            pl.BlockSpec((1, LANES), lambda g, b: (0, g)),
        ],
        out_specs=[
            pl.BlockSpec((rows, LANES), lambda g, b: (b, g)),
            pl.BlockSpec((nsq, 1, 8, LANES), lambda g, b: (b, g, 0, 0)),
        ],
        out_shape=[
            jax.ShapeDtypeStruct((nseq * s, D_SSM), F32),
            jax.ShapeDtypeStruct((nseq, N_GROUP_BLOCKS, 8, LANES), F32),
        ],
        scratch_shapes=[
            pltpu.VMEM((CHUNK * LANES, CHUNK * LANES), BF16),
            pltpu.VMEM((m, lc * LANES), BF16),
            pltpu.VMEM((m, 8, LANES), F32),
            pltpu.VMEM((m, 8, LANES), F32),
            pltpu.VMEM((m, 2 * STATE_LANES), BF16),
        ],
        compiler_params=_cp("arbitrary", "arbitrary"),
        name="s5_scan",
    )(us, h0, dmat, bdz, bdh, acoef, dskip)


def _state_to_blocks(h_re, h_im):
    nseq = h_re.shape[0]
    q = lambda a: a.reshape(nseq, N_GROUP_BLOCKS, 4, LANES)
    return jnp.concatenate([q(h_re), q(h_im)], axis=2)


def _blocks_to_state(hb):
    nseq = hb.shape[0]
    return (hb[:, :, 0:4, :].reshape(nseq, N_SSM_GROUPS, SSM_STATE),
            hb[:, :, 4:8, :].reshape(nseq, N_SSM_GROUPS, SSM_STATE))


HALO = CONV_WIDTH - 1
PAD0 = 32 - HALO


def _conv_kernel(u_ref, st_ref, w_ref, b_ref, g_ref, beta_ref, v_ref, nst_ref, xp_scr, cv_scr, sh_scr, *, rr):
    bs, ts, c = u_ref.shape
    first = pl.program_id(1) == 0
    for b in range(bs):
        @pl.when(first)
        def _():
            xp_scr[b, PAD0:32, :] = st_ref[b]

        xp_scr[b, 32:32 + ts, :] = u_ref[b]

        def body(r, carry):
            base = pl.multiple_of(r * rr, rr)
            for c0 in range(0, c, LANES):
                win = xp_scr[b, pl.ds(base, rr + 32), c0:c0 + LANES]
                acc = jnp.zeros((rr, LANES), F32)
                for ph in range(8):
                    offs = [PAD0 + k for k in range(CONV_WIDTH) if (PAD0 + k) % 8 == ph]
                    span = max(offs) - ph + rr
                    sh_scr[ph, 0:span, :] = win[ph:ph + span]
                    for o in offs:
                        a = o - ph
                        acc = acc + w_ref[o - PAD0:o - PAD0 + 1, c0:c0 + LANES] * sh_scr[ph, a:a + rr, :]
                cv_scr[:, c0:c0 + LANES] = acc
            y = _ln(cv_scr[...] + b_ref[...]) * g_ref[...] + beta_ref[...]
            v_ref[b, pl.ds(base, rr), :] = (y * _sigmoid(y)).astype(v_ref.dtype)
            return carry

        lax.fori_loop(0, ts // rr, body, 0)
        tail = xp_scr[b, ts + PAD0:ts + 32, :]
        nst_ref[b] = tail
        xp_scr[b, PAD0:32, :] = tail


def _conv_call(uc, state, conv_w, conv_b, ln_g, ln_b, bs, ts):
    nseq, s, c = uc.shape
    rr = min(ts, 64)
    vec = lambda a: a.reshape(1, c)
    cst = lambda shape: pl.BlockSpec(shape, lambda i, j: (0,) * len(shape))
    return pl.pallas_call(
        functools.partial(_conv_kernel, rr=rr),
        grid=(nseq // bs, s // ts),
        in_specs=[
            pl.BlockSpec((bs, ts, c), lambda i, j: (i, j, 0)),
            pl.BlockSpec((bs, HALO, c), lambda i, j: (i, 0, 0)),
            cst((CONV_WIDTH, c)), cst((1, c)), cst((1, c)), cst((1, c)),
        ],
        out_specs=[
            pl.BlockSpec((bs, ts, c), lambda i, j: (i, j, 0)),
            pl.BlockSpec((bs, HALO, c), lambda i, j: (i, 0, 0)),
        ],
        out_shape=[
            jax.ShapeDtypeStruct((nseq, s, c), F32),
            jax.ShapeDtypeStruct((nseq, HALO, c), F32),
        ],
        scratch_shapes=[pltpu.VMEM((bs, ts + 32, c), F32), pltpu.VMEM((rr, c), F32),
                        pltpu.VMEM((8, rr + 32, LANES), F32)],
        compiler_params=_cp("parallel", "arbitrary"),
        name="conv_module",
    )(uc, state, conv_w, vec(conv_b), vec(ln_g), vec(ln_b))


def _route(logits):
    lane = lax.broadcasted_iota(jnp.int32, logits.shape, 1).astype(F32)
    neg = jnp.float32(-jnp.inf)
    big = jnp.float32(1e9)
    gmask = jnp.logical_and(lane >= N_EXPERTS, lane < N_EXPERTS + N_EXPERT_GROUPS)
    gmax = jnp.max(jnp.where(gmask, logits, neg), axis=-1, keepdims=True)
    gsel = jnp.min(jnp.where(jnp.logical_and(gmask, logits == gmax), lane, big), axis=-1, keepdims=True)
    p_g = 1.0 / jnp.sum(jnp.where(gmask, jnp.exp(logits - gmax), 0.0), axis=-1, keepdims=True)
    lo = (gsel - N_EXPERTS) * EXPERTS_PER_GROUP
    emask = jnp.logical_and(lane >= lo, lane < lo + EXPERTS_PER_GROUP)
    v1 = jnp.max(jnp.where(emask, logits, neg), axis=-1, keepdims=True)
    i1 = jnp.min(jnp.where(jnp.logical_and(emask, logits == v1), lane, big), axis=-1, keepdims=True)
    emask2 = jnp.logical_and(emask, lane != i1)
    v2 = jnp.max(jnp.where(emask2, logits, neg), axis=-1, keepdims=True)
    i2 = jnp.min(jnp.where(jnp.logical_and(emask2, logits == v2), lane, big), axis=-1, keepdims=True)
    e2 = jnp.exp(v2 - v1)
    den = 1.0 + e2
    w1 = (1.0 / den) * p_g
    w2 = (e2 / den) * p_g
    comb = jnp.where(lane == i1, w1, 0.0) + jnp.where(lane == i2, w2, 0.0)
    return comb + jnp.where(lane == ROUTE_GROUP_LANE, gsel - N_EXPERTS, 0.0)


def _to_row_tiles(ref, x, rows=slice(None)):
    for s in range(x.shape[-1] // LANES):
        ref[rows, s, :] = x[:, s * LANES:(s + 1) * LANES]


def _from_row_tiles(ref, scr):
    for s in range(ref.shape[1]):
        scr[:, s * LANES:(s + 1) * LANES] = ref[:, s, :].astype(scr.dtype)


def _mix_kernel(ys_ref, v_ref, gs_ref, gc_ref, x_ref, g1_ref, sc2_ref, sh2_ref,
                wv_ref, wg_ref, wpw_ref, wo_ref, l1g_ref, l1b_ref, wrh_ref, wrl_ref, br_ref,
                x1_ref, h2_ref, comb_ref):
    bs, ts, d = x_ref.shape
    mh = bs * ts // 2
    for k in range(2):
        if bs > 1:
            bsl, tsl = slice(k * bs // 2, (k + 1) * bs // 2), slice(0, ts)
        else:
            bsl, tsl = slice(0, bs), slice(k * ts // 2, (k + 1) * ts // 2)
        rows = slice(k * mh, (k + 1) * mh)
        ys = ys_ref[rows, :].astype(BF16)
        br_s = _dot(ys, wv_ref[...]) * _sigmoid(_dot(ys, wg_ref[...]))
        br_c = _dot(v_ref[rows, :].astype(BF16), wpw_ref[...])
        mix = gs_ref[rows, :].astype(F32) * br_s + gc_ref[rows, :].astype(F32) * br_c
        x = x_ref[bsl, tsl, :]
        o = _dot(mix.astype(BF16), wo_ref[...]).reshape(x.shape)
        x1 = _ln(ALPHA * x + g1_ref[bsl] * o) * l1g_ref[...] + l1b_ref[...]
        x1_ref[bsl, tsl, :] = x1
        h2 = (_ln(x1) * (1.0 + sc2_ref[bsl]) + sh2_ref[bsl]).reshape(mh, d)
        _to_row_tiles(h2_ref, h2, rows)
        comb_ref[rows, :] = _route(_dot3(h2, wrh_ref[...], wrl_ref[...]) + br_ref[...])


N_MIX_INPUTS = 17


def _mix_kernel_into(*refs):
    _mix_kernel(*refs[:N_MIX_INPUTS], *refs[N_MIX_INPUTS + 1:])


def _mix_call(ys, v, g, x, g1, sc2, sh2, w_val, w_gate, w_pw, w_out, ln_g, ln_b, w_r_hi, w_r_lo, b_r,
              layer, bs, ts, t_all, tok0, h2_buf):
    nseq, s, d = x.shape
    ni, nj = nseq // bs, s // ts
    m = bs * ts
    t = nseq * s
    blk0 = tok0 // m
    row0 = lambda i, j: (i * nj + j, 0)
    mod = pl.BlockSpec((bs, 1, d), lambda i, j: (i, 0, 0))
    res = lambda shape: pl.BlockSpec(shape, lambda i, j: (0,) * len(shape), pipeline_mode=pl.Buffered(1))
    wres = lambda shape: pl.BlockSpec((None,) + shape, lambda i, j: (layer,) + (0,) * len(shape),
                                      pipeline_mode=pl.Buffered(1))
    return pl.pallas_call(
        _mix_kernel if h2_buf is None else _mix_kernel_into,
        grid=(ni, nj),
        in_specs=[
            pl.BlockSpec((m, D_SSM), row0),
            pl.BlockSpec((m, D_CONV), row0),
            pl.BlockSpec((m, d), row0),
            pl.BlockSpec((m, d), lambda i, j: (i * nj + j, 1)),
            pl.BlockSpec((bs, ts, d), lambda i, j: (i, j, 0)),
            mod, mod, mod,
            wres((D_SSM, d)), wres((D_SSM, d)), wres((D_CONV, d)), wres((d, d)),
            res((1, d)), res((1, d)), res((d, LANES)), res((d, LANES)), res((1, LANES)),
        ] + ([] if h2_buf is None else [pl.BlockSpec(memory_space=pl.ANY)]),
        out_specs=[
            pl.BlockSpec((bs, ts, d), lambda i, j: (i, j, 0)),
            pl.BlockSpec((m, d // LANES, LANES), lambda i, j: (blk0 + i * nj + j, 0, 0)),
            pl.BlockSpec((m, LANES), row0),
        ],
        out_shape=[
            jax.ShapeDtypeStruct((nseq, s, d), F32),
            jax.ShapeDtypeStruct((t_all, d // LANES, LANES), F32),
            jax.ShapeDtypeStruct((t, LANES), F32),
        ],
        input_output_aliases={} if h2_buf is None else {N_MIX_INPUTS: 1},
        compiler_params=_cp("parallel", "parallel"),
        name="branch_mix",
    )(ys, v, g, g, x, g1, sc2, sh2, w_val, w_gate, w_pw, w_out,
      ln_g.reshape(1, d), ln_b.reshape(1, d), w_r_hi, w_r_lo, b_r, *([] if h2_buf is None else [h2_buf]))


def _moe_plan(gid, tm):
    t = gid.shape[0]
    npad = t + N_EXPERT_GROUPS * tm
    onehot = (gid[:, None] == jnp.arange(N_EXPERT_GROUPS, dtype=jnp.int32)[None, :]).astype(jnp.int32)
    csum = jnp.cumsum(onehot, axis=0)
    rank = jnp.sum(csum * onehot, axis=1) - 1
    cnt = csum[-1]
    seg = ((cnt + tm - 1) // tm) * tm
    end = jnp.cumsum(seg)
    pos = (end - seg)[gid] + rank
    src = jnp.zeros((npad,), jnp.int32).at[pos].set(jnp.arange(t, dtype=jnp.int32), unique_indices=True)
    tile_start = jnp.arange(npad // tm, dtype=jnp.int32) * tm
    tile_group = jnp.minimum(jnp.sum((tile_start[:, None] >= end[None, :]).astype(jnp.int32), axis=1),
                             N_EXPERT_GROUPS - 1)
    n_tiles = (end[-1] // tm).reshape(1)
    return pos.astype(jnp.int32), src, tile_group.astype(jnp.int32), n_tiles.astype(jnp.int32)


def _gather_kernel(idx_ref, tab_ref, out_ref):
    rows = out_ref.shape[0]
    base = pl.program_id(1) * rows

    def body(r, carry):
        out_ref[r] = tab_ref[idx_ref[base + r]]
        return carry

    lax.fori_loop(0, rows, body, 0, unroll=8)


def _gather_call(table, idx, rows):
    n, nt, lanes = table.shape
    r_total = idx.shape[0]
    half = 8
    return pl.pallas_call(
        _gather_kernel,
        grid_spec=pltpu.PrefetchScalarGridSpec(
            num_scalar_prefetch=1,
            grid=(nt // half, r_total // rows),
            in_specs=[pl.BlockSpec((n, half, lanes), lambda h, i, idx_ref: (0, h, 0),
                                   pipeline_mode=pl.Buffered(1))],
            out_specs=pl.BlockSpec((rows, half, lanes), lambda h, i, idx_ref: (i, h, 0)),
        ),
        out_shape=jax.ShapeDtypeStruct((r_total, nt, lanes), table.dtype),
        compiler_params=_cp("arbitrary", "arbitrary"),
        name="row_gather",
    )(idx, table)


def _experts_kernel(tg_ref, nt_ref, xs_ref, cs_ref, wu_ref, wg_ref, wd_ref, ys_ref, h_scr, act_scr):
    i = pl.program_id(0)
    live = i < nt_ref[0]

    @pl.when(live)
    def _():
        _from_row_tiles(xs_ref, h_scr)
        h = h_scr[...]
        cs = cs_ref[...]
        lane = lax.broadcasted_iota(jnp.int32, cs.shape, 1)
        e0 = tg_ref[i] * EXPERTS_PER_GROUP
        for j in range(EXPERTS_PER_GROUP):
            cols = slice(j * D_EXPERT, (j + 1) * D_EXPERT)
            up = _dot(h, wu_ref[:, cols])
            gt = _dot(h, wg_ref[:, cols])
            ce = jnp.sum(jnp.where(lane == e0 + j, cs, 0.0), axis=-1, keepdims=True)
            act_scr[:, cols] = ((gt * _sigmoid(gt)) * up * ce).astype(BF16)
        _to_row_tiles(ys_ref, _dot(act_scr[...], wd_ref[...]))

    @pl.when(jnp.logical_not(live))
    def _():
        ys_ref[...] = jnp.zeros_like(ys_ref)


def _experts_call(xs, cs, tile_group, n_tiles, w_up, w_gate, w_down, layer, tm):
    npad, nt, lanes = xs.shape
    d = nt * lanes
    gf = EXPERTS_PER_GROUP * D_EXPERT
    one = pl.Buffered(1)
    return pl.pallas_call(
        _experts_kernel,
        grid_spec=pltpu.PrefetchScalarGridSpec(
            num_scalar_prefetch=2,
            grid=(npad // tm,),
            in_specs=[
                pl.BlockSpec((tm, nt, lanes), lambda i, tg, n: (i, 0, 0)),
                pl.BlockSpec((tm, LANES), lambda i, tg, n: (i, 0)),
                pl.BlockSpec((None, d, gf), lambda i, tg, n: (layer, 0, tg[i]), pipeline_mode=one),
                pl.BlockSpec((None, d, gf), lambda i, tg, n: (layer, 0, tg[i]), pipeline_mode=one),
                pl.BlockSpec((None, gf, d), lambda i, tg, n: (layer, tg[i], 0), pipeline_mode=one),
            ],
            out_specs=pl.BlockSpec((tm, nt, lanes), lambda i, tg, n: (i, 0, 0)),
            scratch_shapes=[pltpu.VMEM((tm, d), BF16), pltpu.VMEM((tm, gf), BF16)],
        ),
        out_shape=jax.ShapeDtypeStruct((npad, nt, lanes), F32),
        compiler_params=_cp("arbitrary"),
        name="moe_experts",
    )(tile_group, n_tiles, xs, cs, w_up, w_gate, w_down)


def _final_kernel(ff_ref, x1_ref, g2_ref, l2g_ref, l2b_ref, x2_ref, ff_scr):
    bs, ts, d = x1_ref.shape
    _from_row_tiles(ff_ref, ff_scr)
    ff = ff_scr[...].reshape(bs, ts, d)
    x2_ref[...] = _ln(ALPHA * x1_ref[...] + g2_ref[...] * ff) * l2g_ref[...] + l2b_ref[...]


def _final_call(ff, x1, g2, ln_g, ln_b, tok0, bs, ts):
    nseq, s, d = x1.shape
    ni, nj = nseq // bs, s // ts
    m = bs * ts
    t0 = tok0 // m
    vec = pl.BlockSpec((1, d), lambda i, j: (0, 0))
    return pl.pallas_call(
        _final_kernel,
        grid=(ni, nj),
        in_specs=[
            pl.BlockSpec((m, d // LANES, LANES), lambda i, j: (t0 + i * nj + j, 0, 0)),
            pl.BlockSpec((bs, ts, d), lambda i, j: (i, j, 0)),
            pl.BlockSpec((bs, 1, d), lambda i, j: (i, 0, 0)),
            vec, vec,
        ],
        out_specs=pl.BlockSpec((bs, ts, d), lambda i, j: (i, j, 0)),
        out_shape=jax.ShapeDtypeStruct((nseq, s, d), F32),
        scratch_shapes=[pltpu.VMEM((m, d), F32)],
        compiler_params=_cp("parallel", "parallel"),
        name="moe_residual_ln",
    )(ff, x1, g2, ln_g.reshape(1, d), ln_b.reshape(1, d))


def kernel(x_prompt, x_sample, c_prompt, c_sample, state_ssm_re, state_ssm_im, state_conv, w_ada, b_ada, w_in, ssm_a_re, ssm_a_im, ssm_log_dt, ssm_b_re, ssm_b_im, ssm_c_re, ssm_c_im, ssm_d, w_s5_val, w_s5_gate, conv_w, conv_b, conv_ln_g, conv_ln_b, w_conv_pw, w_out, ln1_g, ln1_b, moe_w_group, moe_b_group, moe_w_router, moe_b_router, moe_w_up, moe_w_gate, moe_w_down, ln2_g, ln2_b):
    n_p, s_p = x_prompt.shape[0], x_prompt.shape[1]
    n_s, s_s = x_sample.shape[0], x_sample.shape[1]

    pad_rows = (-(n_p + n_s)) % 8
    c_all = jnp.concatenate([c_prompt, c_sample, jnp.zeros((pad_rows, D_MODEL), F32)], axis=0)
    ada = _ada_call(c_all, w_ada, b_ada)
    dmat, bdz, bdh, acoef = _s5prep_call(ssm_a_re, ssm_a_im, ssm_log_dt, ssm_b_re, ssm_b_im,
                                         ssm_c_re, ssm_c_im)

    w_inb = _cast_call(w_in, BF16)
    w_val = _cast_call(w_s5_val, BF16)
    w_gate = _cast_call(w_s5_gate, BF16)
    w_pw = _cast_call(w_conv_pw, BF16)
    w_o = _cast_call(w_out, BF16)
    ef = N_EXPERTS * D_EXPERT
    w_up = _cast_experts_call(moe_w_up, BF16)
    w_gt = _cast_experts_call(moe_w_gate, BF16)
    w_dn = _cast_call(moe_w_down.reshape(DEPTH, ef, D_MODEL), BF16)
    zpad = LANES - N_EXPERTS - N_EXPERT_GROUPS
    w_r = jnp.concatenate([moe_w_router, moe_w_group, jnp.zeros((DEPTH, D_MODEL, zpad), F32)], axis=-1)
    w_r_hi = w_r.astype(BF16)
    w_r_lo = (w_r - w_r_hi.astype(F32)).astype(BF16)
    b_r = jnp.concatenate([moe_b_router, moe_b_group, jnp.zeros((DEPTH, zpad), F32)], axis=-1)
    b_r = b_r.reshape(DEPTH, 1, LANES)
    dskip = ssm_d.reshape(DEPTH, 1, D_SSM)

    streams = [
        dict(x=x_prompt, rows=slice(0, n_p), n=n_p, s=s_p, bs=1, ts=512, mbs=1, mts=256,
             h0=jnp.zeros((n_p, N_GROUP_BLOCKS, 8, LANES), F32),
             conv=jnp.zeros((n_p, HALO, D_CONV), F32), cbs=1, cts=512, per_layer_state=False),
        dict(x=x_sample, rows=slice(n_p, n_p + n_s), n=n_s, s=s_s, bs=64, ts=s_s, mbs=32, mts=s_s,
             cbs=8, cts=s_s, per_layer_state=True),
    ]
    outs = [dict(re=[], im=[], conv=[]) for _ in streams]

    t_all = sum(st['n'] * st['s'] for st in streams)
    for l in range(DEPTH):
        mids = []
        h2_all, tok0 = None, 0
        for st, out in zip(streams, outs):
            n, s, bs, ts = st['n'], st['s'], st['bs'], st['ts']
            x = st['x']
            mod = lambda k: ada[l, st['rows'], k * D_MODEL:(k + 1) * D_MODEL].reshape(n, 1, D_MODEL)
            sh1, sc1, g1, sh2, sc2, g2 = (mod(k) for k in range(ADA_CHUNKS))
            if st['per_layer_state']:
                h0 = _state_to_blocks(state_ssm_re[l], state_ssm_im[l])
                cst = state_conv[l]
            else:
                h0, cst = st['h0'], st['conv']

            us, uc, g = _inproj_call(x, sc1, sh1, w_inb, l, bs, ts)
            ys, ht = _s5_call(us, h0, dmat, bdz, bdh, acoef, dskip[l], l, n, s)
            v, nbuf = _conv_call(uc.reshape(n, s, D_CONV), cst, conv_w[l], conv_b[l],
                                 conv_ln_g[l], conv_ln_b[l], st['cbs'], st['cts'])
            x1, h2_all, comb = _mix_call(ys, v.reshape(n * s, D_CONV), g, x, g1, sc2, sh2,
                                         w_val, w_gate, w_pw, w_o, ln1_g[l], ln1_b[l],
                                         w_r_hi[l], w_r_lo[l], b_r[l],
                                         l, st['mbs'], st['mts'], t_all, tok0, h2_all)
            tok0 += n * s
            mids.append((x1, g2, comb))
            h_re, h_im = _blocks_to_state(ht)
            out['re'].append(h_re)
            out['im'].append(h_im)
            out['conv'].append(nbuf)

        comb_all = jnp.concatenate([m_[2] for m_ in mids], axis=0)
        gid = comb_all[:, ROUTE_GROUP_LANE].astype(jnp.int32)
        pos, src, tile_group, n_tiles = _moe_plan(gid, MOE_TILE)
        xs = _gather_call(h2_all, src, GATHER_ROWS)
        ysort = _experts_call(xs, comb_all[src], tile_group, n_tiles, w_up, w_gt, w_dn, l, MOE_TILE)
        ff = _gather_call(ysort, pos, GATHER_ROWS)
        tok0 = 0
        for st, (x1, g2, _) in zip(streams, mids):
            st['x'] = _final_call(ff, x1, g2, ln2_g[l], ln2_b[l], tok0, st['bs'], st['ts'])
            tok0 += st['n'] * st['s']

    stack = lambda xs: jnp.stack(xs, axis=0)
    po, so = outs
    return (streams[0]['x'], streams[1]['x'],
            stack(po['re']), stack(po['im']), stack(po['conv']),
            stack(so['re']), stack(so['im']), stack(so['conv']))
```
